```python
import math, functools
import jax, jax.numpy as jnp
from jax import lax
import numpy as np

D_MODEL = 1024
BATCH = 32
SEQ = 2048
DEPTH = 4
DEC_BATCH = 1
DEC_SEQ = 16384
PAST_LEN = 128

N_EVEN = (DEPTH + 1) // 2
N_ODD = DEPTH // 2

A_HEADS = 4
A_HEAD_DIM = 128
A_WIDTH = A_HEADS * A_HEAD_DIM
A_CHUNK = 64
B_HEADS = 8
B_KV_HEADS = 2
B_GROUP = B_HEADS // B_KV_HEADS
B_HEAD_DIM = 64
B_WIDTH = B_HEADS * B_HEAD_DIM
B_KV_WIDTH = B_KV_HEADS * B_HEAD_DIM
WINDOW = 128
ATTN_BLOCK = 128
AB_IN = 5 * A_WIDTH + B_WIDTH + 2 * B_KV_WIDTH
AB_MIX = A_WIDTH + B_WIDTH
C_WIDTH = 2 * D_MODEL
C_GROUPS = 8
C_GROUP_DIM = C_WIDTH // C_GROUPS
C_CHUNK = 128
FFN_DIM = ((8 * D_MODEL + 3 * 256 - 1) // (3 * 256)) * 256
DEEPNORM_ALPHA = (2.0 * DEPTH) ** 0.25
DEEPNORM_BETA = (8.0 * DEPTH) ** -0.25
LN_EPS = 1e-5
RMS_EPS = 1e-6

kernel_name = 'hgrn2_swa_gmlp_deepnorm_encoder'


def _layernorm(x, g, b):
    xf = x.astype(jnp.float32)
    mu = jnp.mean(xf, axis=-1, keepdims=True)
    var = jnp.mean(jnp.square(xf - mu), axis=-1, keepdims=True)
    return ((xf - mu) * lax.rsqrt(var + LN_EPS) * g.astype(jnp.float32) + b.astype(jnp.float32)).astype(x.dtype)


def _split(z, sizes):
    out, off = [], 0
    for s in sizes:
        out.append(z[..., off:off + s])
        off += s
    return out


def _gla_scan(q, k, v, logf):
    B, L, H, dk = q.shape
    dv = v.shape[-1]
    nc = L // A_CHUNK

    def chunks(a):
        return a.reshape(B, nc, A_CHUNK, H, a.shape[-1]).transpose(1, 0, 3, 2, 4)

    lower = jnp.tril(jnp.ones((A_CHUNK, A_CHUNK), dtype=bool))[:, :, None]

    def step(S, inp):
        qc, kc, vc, gc = inp
        b = jnp.cumsum(gc, axis=2)
        b_last = b[:, :, -1:, :]
        o = jnp.einsum('bhtd,bhde->bhte', qc * jnp.exp(b), S)
        diff = b[:, :, :, None, :] - b[:, :, None, :, :]
        decay = jnp.exp(jnp.where(lower, diff, -jnp.inf))
        scores = jnp.einsum('bhtd,bhtsd->bhts', qc, decay * kc[:, :, None, :, :])
        o = o + jnp.einsum('bhts,bhse->bhte', scores, vc)
        S = jnp.exp(b_last[:, :, 0, :, None]) * S + jnp.einsum('bhsd,bhse->bhde', kc * jnp.exp(b_last - b), vc)
        return S, o

    S0 = jnp.zeros((B, H, dk, dv), jnp.float32)
    _, o = lax.scan(step, S0, (chunks(q), chunks(k), chunks(v), chunks(logf)))
    return o.transpose(1, 0, 3, 2, 4).reshape(B, L, H, dv)


def _hgrn2(q_raw, ff_raw, fb_raw, i_raw, g_raw, lb_f, lb_b, norm_g):
    B, L, _ = q_raw.shape

    def heads(a):
        return a.reshape(B, L, A_HEADS, A_HEAD_DIM).astype(jnp.float32)

    q = jax.nn.silu(heads(q_raw))
    v = heads(i_raw)

    def gates(f_raw, lb):
        f_raw = heads(f_raw)
        lb = lb.astype(jnp.float32).reshape(A_HEADS, A_HEAD_DIM)
        logf = jnp.logaddexp(jnp.log(lb), jnp.log1p(-lb) + jax.nn.log_sigmoid(f_raw))
        k = (1.0 - lb) * jax.nn.sigmoid(-f_raw)
        return k, logf

    k_f, g_f = gates(ff_raw, lb_f)
    k_b, g_b = gates(fb_raw, lb_b)
    flip = lambda a: jnp.flip(a, axis=1)
    o = _gla_scan(q, k_f, v, g_f) + flip(_gla_scan(flip(q), flip(k_b), flip(v), flip(g_b)))
    o = o * lax.rsqrt(jnp.mean(jnp.square(o), axis=-1, keepdims=True) + RMS_EPS) * norm_g.astype(jnp.float32)
    o = o * jax.nn.silu(heads(g_raw))
    return o.reshape(B, L, A_WIDTH)


def _window_gqa(q_raw, k_raw, v_raw, sink):
    B, L, _ = q_raw.shape
    nb = L // ATTN_BLOCK
    q = q_raw.reshape(B, nb, ATTN_BLOCK, B_KV_HEADS, B_GROUP, B_HEAD_DIM).astype(jnp.float32)

    def neighbourhood(a):
        a = a.reshape(B, L, B_KV_HEADS, B_HEAD_DIM).astype(jnp.float32)
        ap = jnp.pad(a, ((0, 0), (ATTN_BLOCK, ATTN_BLOCK), (0, 0), (0, 0)))
        return jnp.concatenate(
            [ap[:, j * ATTN_BLOCK:j * ATTN_BLOCK + L].reshape(B, nb, ATTN_BLOCK, B_KV_HEADS, B_HEAD_DIM)
             for j in range(3)], axis=2)

    kb, vb = neighbourhood(k_raw), neighbourhood(v_raw)
    t = jnp.arange(ATTN_BLOCK)[:, None]
    s = jnp.arange(3 * ATTN_BLOCK)[None, :]
    dist = jnp.abs(t - s + ATTN_BLOCK)
    kpos = jnp.arange(nb)[:, None] * ATTN_BLOCK + jnp.arange(3 * ATTN_BLOCK)[None, :] - ATTN_BLOCK
    valid = (dist <= WINDOW)[None] & ((kpos >= 0) & (kpos < L))[:, None, :]
    slopes = jnp.exp2(-(8.0 / B_HEADS) * jnp.arange(1, B_HEADS + 1, dtype=jnp.float32))
    slopes = slopes.reshape(B_KV_HEADS, B_GROUP)[:, :, None, None]
    scores = jnp.einsum('bnqhgd,bnkhd->bnhgqk', q, kb) * (B_HEAD_DIM ** -0.5)
    scores = scores - slopes * dist.astype(jnp.float32)
    scores = jnp.where(valid[None, :, None, None], scores, -jnp.inf)
    sink_l = sink.astype(jnp.float32).reshape(B_KV_HEADS, B_GROUP)[:, :, None, None]
    m = jnp.maximum(jnp.max(scores, axis=-1, keepdims=True), sink_l)
    p = jnp.exp(scores - m)
    denom = jnp.sum(p, axis=-1, keepdims=True) + jnp.exp(sink_l - m)
    o = jnp.einsum('bnhgqk,bnkhd->bnqhgd', p / denom, vb)
    return o.reshape(B, L, B_WIDTH)


def _sgu(x, w_in, ln_g, ln_b, w_s, b_s, w_out):
    B, L, _ = x.shape
    z = jax.nn.gelu(x @ w_in, approximate=False)
    u, v = jnp.split(z, 2, axis=-1)
    v = _layernorm(v, ln_g, ln_b)
    v = v.reshape(B, L // C_CHUNK, C_CHUNK, C_GROUPS, C_GROUP_DIM)
    s = jnp.einsum('gts,bnsgc->bntgc', w_s, v) + b_s.T[:, :, None]
    return (u * s.reshape(B, L, C_WIDTH)) @ w_out


def _swiglu(x, wg, wu, wd):
    return (jax.nn.silu(x @ wg) * (x @ wu)) @ wd


def _trunk(x, w_in_ab, hgrn_lb_logits, hgrn_norm_g, attn_sink, w_out_ab, w_in_c, c_ln_g, c_ln_b, c_ws, c_bs,
           w_out_c, ffn_w_gate, ffn_w_up, ffn_w_down, ln_mix_g, ln_mix_b, ln_ffn_g, ln_ffn_b):
    p = jax.nn.softmax(hgrn_lb_logits.astype(jnp.float32), axis=0)
    lower_bounds = jnp.maximum(jnp.cumsum(p, axis=0) - p[0:1], 0.0)
    for layer in range(DEPTH):
        j = layer // 2
        if layer % 2 == 0:
            z = x @ w_in_ab[j]
            q_a, ff_a, fb_a, i_a, g_a, q_b, k_b, v_b = _split(
                z, (A_WIDTH,) * 5 + (B_WIDTH, B_KV_WIDTH, B_KV_WIDTH))
            o_a = _hgrn2(q_a, ff_a, fb_a, i_a, g_a, lower_bounds[j, 0], lower_bounds[j, 1], hgrn_norm_g[j])
            o_b = _window_gqa(q_b, k_b, v_b, attn_sink[j])
            y = jnp.concatenate([o_a.astype(x.dtype), o_b.astype(x.dtype)], axis=-1) @ w_out_ab[j]
        else:
            y = _sgu(x, w_in_c[j], c_ln_g[j], c_ln_b[j], c_ws[j], c_bs[j], w_out_c[j])
        x = _layernorm(DEEPNORM_ALPHA * x + y, ln_mix_g[layer], ln_mix_b[layer])
        x = _layernorm(DEEPNORM_ALPHA * x + _swiglu(x, ffn_w_gate[layer], ffn_w_up[layer], ffn_w_down[layer]),
                       ln_ffn_g[layer], ln_ffn_b[layer])
    return x


def setup_inputs(seed: int = 0) -> dict:
    key = jax.random.key(seed)
    ks = jax.random.split(key, 20)

    def nrm(k, shape, scale):
        return jax.random.normal(k, shape, jnp.float32) * scale

    return {
        'x_prompt': nrm(ks[0], (BATCH, SEQ, D_MODEL), 1.0),
        'x_sample': nrm(ks[1], (DEC_BATCH, DEC_SEQ, D_MODEL), 1.0),
        'w_in_ab': nrm(ks[2], (N_EVEN, D_MODEL, AB_IN), D_MODEL ** -0.5),
        'hgrn_lb_logits': nrm(ks[3], (N_EVEN, 2, A_WIDTH), 0.5),
        'hgrn_norm_g': 1.0 + nrm(ks[4], (N_EVEN, A_HEAD_DIM), 0.02),
        'attn_sink': nrm(ks[5], (N_EVEN, B_HEADS), 0.5),
        'w_out_ab': nrm(ks[6], (N_EVEN, AB_MIX, D_MODEL), AB_MIX ** -0.5 * DEEPNORM_BETA),
        'w_in_c': nrm(ks[7], (N_ODD, D_MODEL, 2 * C_WIDTH), D_MODEL ** -0.5),
        'c_ln_g': 1.0 + nrm(ks[8], (N_ODD, C_WIDTH), 0.02),
        'c_ln_b': nrm(ks[9], (N_ODD, C_WIDTH), 0.02),
        'c_ws': nrm(ks[10], (N_ODD, C_GROUPS, C_CHUNK, C_CHUNK), C_CHUNK ** -0.5),
        'c_bs': 1.0 + nrm(ks[11], (N_ODD, C_GROUPS, C_CHUNK), 0.02),
        'w_out_c': nrm(ks[12], (N_ODD, C_WIDTH, D_MODEL), C_WIDTH ** -0.5 * DEEPNORM_BETA),
        'ffn_w_gate': nrm(ks[13], (DEPTH, D_MODEL, FFN_DIM), D_MODEL ** -0.5),
        'ffn_w_up': nrm(ks[14], (DEPTH, D_MODEL, FFN_DIM), D_MODEL ** -0.5 * DEEPNORM_BETA),
        'ffn_w_down': nrm(ks[15], (DEPTH, FFN_DIM, D_MODEL), FFN_DIM ** -0.5 * DEEPNORM_BETA),
        'ln_mix_g': 1.0 + nrm(ks[16], (DEPTH, D_MODEL), 0.02),
        'ln_mix_b': nrm(ks[17], (DEPTH, D_MODEL), 0.02),
        'ln_ffn_g': 1.0 + nrm(ks[18], (DEPTH, D_MODEL), 0.02),
        'ln_ffn_b': nrm(ks[19], (DEPTH, D_MODEL), 0.02),
    }


def reference(x_prompt, x_sample, w_in_ab, hgrn_lb_logits, hgrn_norm_g, attn_sink, w_out_ab, w_in_c, c_ln_g,
              c_ln_b, c_ws, c_bs, w_out_c, ffn_w_gate, ffn_w_up, ffn_w_down, ln_mix_g, ln_mix_b, ln_ffn_g,
              ln_ffn_b):
    trunk = functools.partial(
        _trunk, w_in_ab=w_in_ab, hgrn_lb_logits=hgrn_lb_logits, hgrn_norm_g=hgrn_norm_g, attn_sink=attn_sink,
        w_out_ab=w_out_ab, w_in_c=w_in_c, c_ln_g=c_ln_g, c_ln_b=c_ln_b, c_ws=c_ws, c_bs=c_bs, w_out_c=w_out_c,
        ffn_w_gate=ffn_w_gate, ffn_w_up=ffn_w_up, ffn_w_down=ffn_w_down, ln_mix_g=ln_mix_g, ln_mix_b=ln_mix_b,
        ln_ffn_g=ln_ffn_g, ln_ffn_b=ln_ffn_b)
    y_prompt = trunk(x_prompt)
    y_sample = trunk(x_sample)
    return (y_prompt, y_sample)
```

```python
import functools
import math

import jax
import jax.numpy as jnp
from jax import lax
from jax.experimental import pallas as pl
from jax.experimental.pallas import tpu as pltpu

D_MODEL = 1024
DEPTH = 4
A_HEADS = 4
A_HEAD_DIM = 128
A_WIDTH = A_HEADS * A_HEAD_DIM
B_HEADS = 8
B_KV_HEADS = 2
B_GROUP = B_HEADS // B_KV_HEADS
B_HEAD_DIM = 64
B_WIDTH = B_HEADS * B_HEAD_DIM
B_KV_WIDTH = B_KV_HEADS * B_HEAD_DIM
WINDOW = 128
ATTN_BLOCK = 128
AB_IN = 5 * A_WIDTH + B_WIDTH + 2 * B_KV_WIDTH
C_WIDTH = 2 * D_MODEL
C_GROUPS = 8
C_GROUP_DIM = C_WIDTH // C_GROUPS
C_CHUNK = 128
FFN_DIM = 2816
DEEPNORM_ALPHA = (2.0 * DEPTH) ** 0.25
LN_EPS = 1e-5
RMS_EPS = 1e-6

LANE = 128
MXU_DIM = 256
VMEM_LIMIT = 56 * 1024 * 1024

TOKEN_TILE = 512
SCAN_TILE = 512
SCAN_CHUNK = 128
ATTN_TILE = 512

F32 = jnp.float32
BF16 = jnp.bfloat16


def _dot(a, b):
    return jnp.dot(a, b, preferred_element_type=F32)


def _dot_nt(a, b):
    return lax.dot_general(a, b, (((1,), (1,)), ((), ())), preferred_element_type=F32)


def _dot_tn(a, b):
    return lax.dot_general(a, b, (((0,), (0,)), ((), ())), preferred_element_type=F32)


def _layernorm(x, g, b):
    mu = jnp.mean(x, axis=-1, keepdims=True)
    xc = x - mu
    var = jnp.mean(xc * xc, axis=-1, keepdims=True)
    return xc * lax.rsqrt(var + LN_EPS) * g + b


def _resident(shape):
    nd = len(shape)
    return pl.BlockSpec(shape, lambda *_: (0,) * nd, pipeline_mode=pl.Buffered(1))


def _params(*sem):
    return pltpu.CompilerParams(dimension_semantics=sem, vmem_limit_bytes=VMEM_LIMIT)


def _inproj_body(x_ref, w_ref, z_ref):
    xb = x_ref[...].astype(BF16)
    for c in range(AB_IN // MXU_DIM):
        sl = slice(c * MXU_DIM, (c + 1) * MXU_DIM)
        z_ref[:, sl] = _dot(xb, w_ref[:, sl])


def _inproj(x, w):
    n = x.shape[0]
    return pl.pallas_call(
        _inproj_body,
        grid=(n // TOKEN_TILE,),
        in_specs=[pl.BlockSpec((TOKEN_TILE, D_MODEL), lambda i: (i, 0)), _resident((D_MODEL, AB_IN))],
        out_specs=pl.BlockSpec((TOKEN_TILE, AB_IN), lambda i: (i, 0)),
        out_shape=jax.ShapeDtypeStruct((n, AB_IN), F32),
        compiler_params=_params("parallel"),
        name="inproj",
    )(x, w)


def _hgrn_chunk(q_raw, f_raw, v, gc, st, tri, last_row, mid_row):
    log_lb, l1m, oml = gc[0:1], gc[1:2], gc[2:3]
    e = jnp.exp(-jnp.abs(f_raw))
    log_sig = jnp.minimum(f_raw, 0.0) - jnp.log1p(e)
    c = l1m + log_sig
    logf = jnp.maximum(log_lb, c) + jnp.log1p(jnp.exp(-jnp.abs(log_lb - c)))
    k = oml * (jnp.where(f_raw >= 0.0, e, 1.0) / (1.0 + e))
    q = q_raw * jax.nn.sigmoid(q_raw)

    h1 = logf.astype(BF16)
    r1 = logf - h1.astype(F32)
    h2 = r1.astype(BF16)
    h3 = (r1 - h2.astype(F32)).astype(BF16)
    cum = tri.astype(BF16)
    b = _dot(cum, h1) + _dot(cum, h2) + _dot(cum, h3)

    b_last = b[last_row:last_row + 1]
    b_mid = b[mid_row:mid_row + 1]
    q_in = (q * jnp.exp(b - b_mid)).astype(BF16)
    k_in = (k * jnp.exp(b_mid - b)).astype(BF16)
    q_st = (q * jnp.exp(b)).astype(BF16)
    k_st = (k * jnp.exp(b_last - b)).astype(BF16)
    vb = v.astype(BF16)

    a = jnp.where(tri, _dot_nt(q_in, k_in), 0.0).astype(BF16)
    o = _dot(a, vb) + _dot_nt(q_st, st.astype(BF16))
    st_new = st * jnp.exp(b_last) + _dot_tn(vb, k_st)
    return o, st_new


def _hgrn_body(qf_ref, ff_ref, vf_ref, qb_ref, fb_ref, vb_ref, gc_ref, of_ref, ob_ref, sf_ref, sb_ref):
    @pl.when(pl.program_id(2) == 0)
    def _():
        sf_ref[...] = jnp.zeros_like(sf_ref)
        sb_ref[...] = jnp.zeros_like(sb_ref)

    cs = SCAN_CHUNK
    row = lax.broadcasted_iota(jnp.int32, (cs, cs), 0)
    col = lax.broadcasted_iota(jnp.int32, (cs, cs), 1)
    tri_f = row >= col
    tri_b = row <= col
    gc_f = gc_ref[0]
    gc_b = gc_ref[1]
    n_chunks = SCAN_TILE // cs
    sf = sf_ref[...]
    sb = sb_ref[...]
    for j in range(n_chunks):
        rf = slice(j * cs, (j + 1) * cs)
        o, sf = _hgrn_chunk(qf_ref[rf, :], ff_ref[rf, :], vf_ref[rf, :], gc_f, sf, tri_f, cs - 1, cs // 2 - 1)
        of_ref[rf, :] = o
        jb = n_chunks - 1 - j
        rb = slice(jb * cs, (jb + 1) * cs)
        o, sb = _hgrn_chunk(qb_ref[rb, :], fb_ref[rb, :], vb_ref[rb, :], gc_b, sb, tri_b, 0, cs // 2)
        ob_ref[rb, :] = o
    sf_ref[...] = sf
    sb_ref[...] = sb


def _hgrn(z, gate_consts, batch, seq):
    nt = seq // SCAN_TILE
    hw = A_HEADS

    def fwd(group):
        return pl.BlockSpec((SCAN_TILE, A_HEAD_DIM), lambda b, h, i: (b * nt + i, group * hw + h))

    def bwd(group):
        return pl.BlockSpec((SCAN_TILE, A_HEAD_DIM), lambda b, h, i: (b * nt + nt - 1 - i, group * hw + h))

    out = jax.ShapeDtypeStruct((batch * seq, A_WIDTH), F32)
    return pl.pallas_call(
        _hgrn_body,
        grid=(batch, A_HEADS, nt),
        in_specs=[fwd(0), fwd(1), fwd(3), bwd(0), bwd(2), bwd(3),
                  pl.BlockSpec((2, 3, A_HEAD_DIM), lambda b, h, i: (0, 0, h))],
        out_specs=[pl.BlockSpec((SCAN_TILE, A_HEAD_DIM), lambda b, h, i: (b * nt + i, h)),
                   pl.BlockSpec((SCAN_TILE, A_HEAD_DIM), lambda b, h, i: (b * nt + nt - 1 - i, h))],
        out_shape=[out, out],
        scratch_shapes=[pltpu.VMEM((A_HEAD_DIM, A_HEAD_DIM), F32), pltpu.VMEM((A_HEAD_DIM, A_HEAD_DIM), F32)],
        compiler_params=_params("parallel", "parallel", "arbitrary"),
        name="hgrn2_scan",
    )(z, z, z, z, z, z, gate_consts)


def _attn_body(sink_ref, q_ref, kc_ref, kp_ref, kn_ref, vc_ref, vp_ref, vn_ref, o_ref, k_scr, v_scr):
    i = pl.program_id(1)
    first = i == 0
    last = i == pl.num_programs(1) - 1
    blk = ATTN_BLOCK
    k_scr[0:blk, :] = kp_ref[...]
    k_scr[blk:blk + ATTN_TILE, :] = kc_ref[...]
    k_scr[blk + ATTN_TILE:, :] = kn_ref[...]
    v_scr[0:blk, :] = vp_ref[...]
    v_scr[blk:blk + ATTN_TILE, :] = vc_ref[...]
    v_scr[blk + ATTN_TILE:, :] = vn_ref[...]

    t = lax.broadcasted_iota(jnp.int32, (blk, 3 * blk), 0)
    s = lax.broadcasted_iota(jnp.int32, (blk, 3 * blk), 1)
    dist = jnp.abs(t - s + blk)
    in_window = dist <= WINDOW
    dist_f = dist.astype(F32)
    n_blocks = ATTN_TILE // blk
    key_lo = jnp.where(first, blk, 0)
    key_hi = jnp.where(last, 2 * blk, 3 * blk)
    for j in range(n_blocks):
        valid = in_window
        if j == 0:
            valid = valid & (s >= key_lo)
        if j == n_blocks - 1:
            valid = valid & (s < key_hi)
        rows = slice(j * blk, (j + 1) * blk)
        k_nb = k_scr[j * blk:(j + 3) * blk, :]
        v_nb = v_scr[j * blk:(j + 3) * blk, :]
        for kvh in range(B_KV_HEADS):
            kh = k_nb[:, kvh * B_HEAD_DIM:(kvh + 1) * B_HEAD_DIM].astype(BF16)
            vh = v_nb[:, kvh * B_HEAD_DIM:(kvh + 1) * B_HEAD_DIM].astype(BF16)
            for g in range(B_GROUP):
                head = kvh * B_GROUP + g
                slope = 2.0 ** (-(8.0 / B_HEADS) * (head + 1))
                cols = slice(head * B_HEAD_DIM, (head + 1) * B_HEAD_DIM)
                qh = q_ref[rows, cols].astype(BF16)
                sc = _dot_nt(qh, kh) * (B_HEAD_DIM ** -0.5) - slope * dist_f
                sc = jnp.where(valid, sc, -jnp.inf)
                sink = sink_ref[head]
                m = jnp.maximum(jnp.max(sc, axis=-1, keepdims=True), sink)
                p = jnp.exp(sc - m)
                denom = jnp.sum(p, axis=-1, keepdims=True) + jnp.exp(sink - m)
                o_ref[rows, cols] = _dot(p.astype(BF16), vh) / denom


def _attn(z, sink, batch, seq):
    nt = seq // ATTN_TILE
    per = ATTN_TILE // ATTN_BLOCK
    nblk = seq // ATTN_BLOCK
    q_col = 5 * A_WIDTH // B_WIDTH
    k_col = (5 * A_WIDTH + B_WIDTH) // B_KV_WIDTH
    v_col = k_col + 1

    def cur(col):
        return pl.BlockSpec((ATTN_TILE, B_KV_WIDTH), lambda b, i: (b * nt + i, col))

    def prev(col):
        return pl.BlockSpec((ATTN_BLOCK, B_KV_WIDTH),
                            lambda b, i: (b * nblk + jnp.maximum(i * per - 1, 0), col))

    def nxt(col):
        return pl.BlockSpec((ATTN_BLOCK, B_KV_WIDTH),
                            lambda b, i: (b * nblk + jnp.minimum((i + 1) * per, nblk - 1), col))

    return pl.pallas_call(
        _attn_body,
        grid=(batch, nt),
        in_specs=[pl.BlockSpec(memory_space=pltpu.SMEM),
                  pl.BlockSpec((ATTN_TILE, B_WIDTH), lambda b, i: (b * nt + i, q_col)),
                  cur(k_col), prev(k_col), nxt(k_col), cur(v_col), prev(v_col), nxt(v_col)],
        out_specs=pl.BlockSpec((ATTN_TILE, B_WIDTH), lambda b, i: (b * nt + i, 0)),
        out_shape=jax.ShapeDtypeStruct((batch * seq, B_WIDTH), F32),
        scratch_shapes=[pltpu.VMEM((ATTN_TILE + 2 * ATTN_BLOCK, B_KV_WIDTH), F32),
                        pltpu.VMEM((ATTN_TILE + 2 * ATTN_BLOCK, B_KV_WIDTH), F32)],
        compiler_params=_params("parallel", "parallel"),
        name="window_attn",
    )(sink, z, z, z, z, z, z, z)


def _outproj_body(x_ref, of_ref, ob_ref, g_ref, oattn_ref, ng_ref, w_ref, lg_ref, lb_ref, y_ref):
    o = of_ref[...] + ob_ref[...]
    parts = []
    for h in range(A_HEADS):
        sl = slice(h * A_HEAD_DIM, (h + 1) * A_HEAD_DIM)
        oh = o[:, sl]
        oh = oh * lax.rsqrt(jnp.mean(oh * oh, axis=-1, keepdims=True) + RMS_EPS) * ng_ref[...]
        gh = g_ref[:, sl]
        parts.append((oh * (gh * jax.nn.sigmoid(gh))).astype(BF16))
    o_a = jnp.concatenate(parts, axis=-1)
    y = _dot(o_a, w_ref[0:A_WIDTH, :]) + _dot(oattn_ref[...].astype(BF16), w_ref[A_WIDTH:, :])
    y_ref[...] = _layernorm(DEEPNORM_ALPHA * x_ref[...] + y, lg_ref[...], lb_ref[...])


def _outproj(x, o_f, o_b, z, o_attn, norm_g, w, ln_g, ln_b):
    n = x.shape[0]
    tm = TOKEN_TILE
    g_col = 4 * A_WIDTH // A_WIDTH
    return pl.pallas_call(
        _outproj_body,
        grid=(n // tm,),
        in_specs=[pl.BlockSpec((tm, D_MODEL), lambda i: (i, 0)),
                  pl.BlockSpec((tm, A_WIDTH), lambda i: (i, 0)),
                  pl.BlockSpec((tm, A_WIDTH), lambda i: (i, 0)),
                  pl.BlockSpec((tm, A_WIDTH), lambda i: (i, g_col)),
                  pl.BlockSpec((tm, B_WIDTH), lambda i: (i, 0)),
                  _resident((1, A_HEAD_DIM)), _resident((A_WIDTH + B_WIDTH, D_MODEL)),
                  _resident((1, D_MODEL)), _resident((1, D_MODEL))],
        out_specs=pl.BlockSpec((tm, D_MODEL), lambda i: (i, 0)),
        out_shape=jax.ShapeDtypeStruct((n, D_MODEL), F32),
        compiler_params=_params("parallel"),
        name="outproj_ln",
    )(x, o_f, o_b, z, o_attn, norm_g, w, ln_g, ln_b)


def _sgu_body(x_ref, win_ref, cg_ref, cb_ref, ws_ref, bs_ref, wout_ref, lg_ref, lb_ref, y_ref,
              u_scr, v_scr, h_scr):
    tm = TOKEN_TILE
    xb = x_ref[...].astype(BF16)
    step = 2 * MXU_DIM
    for c in range(2 * C_WIDTH // step):
        z = _dot(xb, win_ref[:, c * step:(c + 1) * step])
        z = 0.5 * z * (1.0 + lax.erf(z * (2.0 ** -0.5)))
        if c < C_WIDTH // step:
            u_scr[:, c * step:(c + 1) * step] = z
        else:
            v_scr[:, c * step - C_WIDTH:(c + 1) * step - C_WIDTH] = z
    v = _layernorm(v_scr[...], cg_ref[...], cb_ref[...])
    v_scr[...] = v
    for g in range(C_GROUPS):
        w_g = ws_ref[g]
        bias = bs_ref[:, g:g + 1]
        cols = slice(g * C_GROUP_DIM, (g + 1) * C_GROUP_DIM)
        for n in range(tm // C_CHUNK):
            rows = slice(n * C_CHUNK, (n + 1) * C_CHUNK)
            s = _dot(w_g, v_scr[rows, cols].astype(BF16)) + bias
            h_scr[rows, cols] = (u_scr[rows, cols] * s).astype(BF16)
    y = _dot(h_scr[...], wout_ref[...])
    y_ref[...] = _layernorm(DEEPNORM_ALPHA * x_ref[...] + y, lg_ref[...], lb_ref[...])


def _sgu(x, w_in, c_g, c_b, w_s, b_s_t, w_out, ln_g, ln_b):
    n = x.shape[0]
    tm = TOKEN_TILE
    return pl.pallas_call(
        _sgu_body,
        grid=(n // tm,),
        in_specs=[pl.BlockSpec((tm, D_MODEL), lambda i: (i, 0)),
                  _resident((D_MODEL, 2 * C_WIDTH)), _resident((1, C_WIDTH)), _resident((1, C_WIDTH)),
                  _resident((C_GROUPS, C_CHUNK, C_CHUNK)), _resident((C_CHUNK, C_GROUPS)),
                  _resident((C_WIDTH, D_MODEL)), _resident((1, D_MODEL)), _resident((1, D_MODEL))],
        out_specs=pl.BlockSpec((tm, D_MODEL), lambda i: (i, 0)),
        out_shape=jax.ShapeDtypeStruct((n, D_MODEL), F32),
        scratch_shapes=[pltpu.VMEM((tm, C_WIDTH), F32), pltpu.VMEM((tm, C_WIDTH), F32),
                        pltpu.VMEM((tm, C_WIDTH), BF16)],
        compiler_params=_params("parallel"),
        name="sgu_ln",
    )(x, w_in, c_g, c_b, w_s, b_s_t, w_out, ln_g, ln_b)


def _ffn_body(x_ref, wg_ref, wu_ref, wd_ref, lg_ref, lb_ref, y_ref, h_scr):
    xb = x_ref[...].astype(BF16)
    for c in range(FFN_DIM // MXU_DIM):
        sl = slice(c * MXU_DIM, (c + 1) * MXU_DIM)
        gate = _dot(xb, wg_ref[:, sl])
        up = _dot(xb, wu_ref[:, sl])
        h_scr[:, sl] = (gate * jax.nn.sigmoid(gate) * up).astype(BF16)
    y = _dot(h_scr[...], wd_ref[...])
    y_ref[...] = _layernorm(DEEPNORM_ALPHA * x_ref[...] + y, lg_ref[...], lb_ref[...])


def _ffn(x, wg, wu, wd, ln_g, ln_b):
    n = x.shape[0]
    tm = TOKEN_TILE
    return pl.pallas_call(
        _ffn_body,
        grid=(n // tm,),
        in_specs=[pl.BlockSpec((tm, D_MODEL), lambda i: (i, 0)),
                  _resident((D_MODEL, FFN_DIM)), _resident((D_MODEL, FFN_DIM)), _resident((FFN_DIM, D_MODEL)),
                  _resident((1, D_MODEL)), _resident((1, D_MODEL))],
        out_specs=pl.BlockSpec((tm, D_MODEL), lambda i: (i, 0)),
        out_shape=jax.ShapeDtypeStruct((n, D_MODEL), F32),
        scratch_shapes=[pltpu.VMEM((tm, FFN_DIM), BF16)],
        compiler_params=_params("parallel"),
        name="swiglu_ln",
    )(x, wg, wu, wd, ln_g, ln_b)


def _trunk(x, p):
    batch, seq, _ = x.shape
    x = x.reshape(batch * seq, D_MODEL)
    for layer in range(DEPTH):
        j = layer // 2
        if layer % 2 == 0:
            z = _inproj(x, p["w_in_ab"][j])
            o_f, o_b = _hgrn(z, p["gate_consts"][j], batch, seq)
            o_attn = _attn(z, p["attn_sink"][j], batch, seq)
            x = _outproj(x, o_f, o_b, z, o_attn, p["hgrn_norm_g"][j], p["w_out_ab"][j],
                         p["ln_mix_g"][layer], p["ln_mix_b"][layer])
        else:
            x = _sgu(x, p["w_in_c"][j], p["c_ln_g"][j], p["c_ln_b"][j], p["c_ws"][j], p["c_bs_t"][j],
                     p["w_out_c"][j], p["ln_mix_g"][layer], p["ln_mix_b"][layer])
        x = _ffn(x, p["ffn_w_gate"][layer], p["ffn_w_up"][layer], p["ffn_w_down"][layer],
                 p["ln_ffn_g"][layer], p["ln_ffn_b"][layer])
    return x.reshape(batch, seq, D_MODEL)


def kernel(x_prompt, x_sample, w_in_ab, hgrn_lb_logits, hgrn_norm_g, attn_sink, w_out_ab, w_in_c, c_ln_g,
           c_ln_b, c_ws, c_bs, w_out_c, ffn_w_gate, ffn_w_up, ffn_w_down, ln_mix_g, ln_mix_b, ln_ffn_g,
           ln_ffn_b):
    prob = jax.nn.softmax(hgrn_lb_logits.astype(F32), axis=0)
    lower = jnp.maximum(jnp.cumsum(prob, axis=0) - prob[0:1], 0.0)
    gate_consts = jnp.stack([jnp.log(lower), jnp.log1p(-lower), 1.0 - lower], axis=2)
    row = lambda a: a.astype(F32)[:, None, :]
    p = dict(
        w_in_ab=w_in_ab.astype(BF16), gate_consts=gate_consts, hgrn_norm_g=row(hgrn_norm_g),
        attn_sink=attn_sink.astype(F32), w_out_ab=w_out_ab.astype(BF16),
        w_in_c=w_in_c.astype(BF16), c_ln_g=row(c_ln_g), c_ln_b=row(c_ln_b), c_ws=c_ws.astype(BF16),
        c_bs_t=jnp.swapaxes(c_bs.astype(F32), 1, 2), w_out_c=w_out_c.astype(BF16),
        ffn_w_gate=ffn_w_gate.astype(BF16), ffn_w_up=ffn_w_up.astype(BF16), ffn_w_down=ffn_w_down.astype(BF16),
        ln_mix_g=row(ln_mix_g), ln_mix_b=row(ln_mix_b), ln_ffn_g=row(ln_ffn_g), ln_ffn_b=row(ln_ffn_b),
    )
    return (_trunk(x_prompt, p), _trunk(x_sample, p))
```

```python
import functools
import math

import numpy as np
import jax
import jax.numpy as jnp
from jax import lax
from jax.experimental import pallas as pl
from jax.experimental.pallas import tpu as pltpu

D_MODEL = 1024
DEPTH = 4
A_HEADS = 4
A_HEAD_DIM = 128
A_WIDTH = A_HEADS * A_HEAD_DIM
B_HEADS = 8
B_KV_HEADS = 2
B_GROUP = B_HEADS // B_KV_HEADS
B_HEAD_DIM = 64
B_WIDTH = B_HEADS * B_HEAD_DIM
B_KV_WIDTH = B_KV_HEADS * B_HEAD_DIM
WINDOW = 128
ATTN_BLOCK = 128
AB_IN = 5 * A_WIDTH + B_WIDTH + 2 * B_KV_WIDTH
C_WIDTH = 2 * D_MODEL
C_GROUPS = 8
C_GROUP_DIM = C_WIDTH // C_GROUPS
C_CHUNK = 128
FFN_DIM = 2816
DEEPNORM_ALPHA = (2.0 * DEPTH) ** 0.25
LN_EPS = 1e-5
RMS_EPS = 1e-6

LANE = 128
MXU_DIM = 256
VMEM_LIMIT = 56 * 1024 * 1024

TOKEN_TILE = 512
SCAN_TILE = 1024
SCAN_CHUNK = 128
ATTN_TILE = 512

F32 = jnp.float32
BF16 = jnp.bfloat16
LOG2E = math.log2(math.e)


def _dot(a, b):
    return jnp.dot(a, b, preferred_element_type=F32)


def _dot_nt(a, b):
    return lax.dot_general(a, b, (((1,), (1,)), ((), ())), preferred_element_type=F32)


def _dot_tn(a, b):
    return lax.dot_general(a, b, (((0,), (0,)), ((), ())), preferred_element_type=F32)


def _layernorm(x, g, b):
    mu = jnp.mean(x, axis=-1, keepdims=True)
    xc = x - mu
    var = jnp.mean(xc * xc, axis=-1, keepdims=True)
    return xc * lax.rsqrt(var + LN_EPS) * g + b


def _resident(shape):
    nd = len(shape)
    return pl.BlockSpec(shape, lambda *_: (0,) * nd, pipeline_mode=pl.Buffered(1))


def _params(*sem):
    return pltpu.CompilerParams(dimension_semantics=sem, vmem_limit_bytes=VMEM_LIMIT)


def _inproj_body(x_ref, w_ref, z_ref):
    xb = x_ref[...].astype(BF16)
    for c in range(AB_IN // MXU_DIM):
        sl = slice(c * MXU_DIM, (c + 1) * MXU_DIM)
        z_ref[:, sl] = _dot(xb, w_ref[:, sl])


def _inproj(x, w):
    n = x.shape[0]
    return pl.pallas_call(
        _inproj_body,
        grid=(n // TOKEN_TILE,),
        in_specs=[pl.BlockSpec((TOKEN_TILE, D_MODEL), lambda i: (i, 0)), _resident((D_MODEL, AB_IN))],
        out_specs=pl.BlockSpec((TOKEN_TILE, AB_IN), lambda i: (i, 0)),
        out_shape=jax.ShapeDtypeStruct((n, AB_IN), F32),
        compiler_params=_params("parallel"),
        name="inproj",
    )(x, w)


N_LEVELS = SCAN_CHUNK.bit_length() - 1
SUBLANES = 8
SCAN_HEADS = 2


def _scan_tables():
    c = SCAN_CHUNK
    t = np.arange(c)[:, None]
    u = np.arange(c)[None, :]
    level = np.where(u == t, 0, -1)
    for lvl in range(1, N_LEVELS + 1):
        h = c >> lvl
        level = np.where((u < t) & ((t ^ u) >= h) & ((t ^ u) < 2 * h), lvl, level)
    cum = np.stack([u <= t, u >= t]).astype(np.float32)
    cum2 = jnp.asarray(np.concatenate([cum, cum], axis=2), dtype=BF16)
    lv = jnp.asarray(np.stack([level, level.T]).astype(np.int32))
    return cum2, lv


def _neg_abs(x):
    bits = lax.bitcast_convert_type(x, jnp.uint32) | jnp.uint32(0x80000000)
    return lax.bitcast_convert_type(bits, F32)


def _level_exponents(lf2, xb, xb_ref, lvl, rev):
    cs = SCAN_CHUNK
    h = cs >> lvl
    if h == 1:
        row = lax.broadcasted_iota(jnp.int32, lf2.shape, 0)
        query_row = (row & 1) == (0 if rev else 1)
        return jnp.where(query_row, lf2, 0.0)
    pieces = []
    if 2 * h >= SUBLANES:
        for start in range(0, cs, 2 * h):
            bnd = start + (h if rev else h - 1)
            pieces.append(_neg_abs(xb[start:start + 2 * h] - xb_ref[bnd:bnd + 1, :]))
    else:
        sub = lax.broadcasted_iota(jnp.int32, (SUBLANES, lf2.shape[1]), 0)
        for start in range(0, cs, SUBLANES):
            bounds = [p + (h if rev else h - 1) for p in range(start, start + SUBLANES, 2 * h)]
            g = xb_ref[bounds[-1]:bounds[-1] + 1, :]
            for n in range(len(bounds) - 2, -1, -1):
                g = jnp.where(sub < (n + 1) * 2 * h, xb_ref[bounds[n]:bounds[n] + 1, :], g)
            pieces.append(_neg_abs(xb[start:start + SUBLANES] - g))
    return jnp.concatenate(pieces, axis=0)


def _hgrn_chunk(q_raw, f_raw, v, gc, st, cum2, level, rev, xb_ref):
    cs = SCAN_CHUNK
    log_lb, l1m = gc[0:1], gc[1:2]
    log_sig = jnp.minimum(f_raw, 0.0) - jnp.log1p(jnp.exp(-jnp.abs(f_raw)))
    c = l1m + log_sig
    logf = jnp.maximum(log_lb, c) + jnp.log1p(jnp.exp(-jnp.abs(log_lb - c)))
    lf2 = logf * LOG2E
    lk2 = (c - f_raw) * LOG2E
    k = jnp.exp2(lk2)
    q = q_raw * jax.nn.sigmoid(q_raw)
    vb = v.astype(BF16)

    h1 = lf2.astype(BF16)
    h2 = (lf2 - h1.astype(F32)).astype(BF16)
    xb = _dot(cum2, jnp.concatenate([h1, h2], axis=0))
    xb_ref[...] = xb
    last = 0 if rev else cs - 1
    x_last = xb_ref[last:last + 1, :]

    a = jnp.where(level == 0, _dot_nt(q.astype(BF16), k.astype(BF16)), 0.0)
    for lvl in range(1, N_LEVELS + 1):
        p = jnp.exp2(_level_exponents(lf2, xb, xb_ref, lvl, rev))
        a = jnp.where(level == lvl, _dot_nt((q * p).astype(BF16), (k * p).astype(BF16)), a)

    q_st = (q * jnp.exp2(xb)).astype(BF16)
    k_st = jnp.exp2(_neg_abs(xb - x_last) + lk2).astype(BF16)
    o = _dot(a.astype(BF16), vb) + _dot_nt(q_st, st.astype(BF16))
    st_new = st * jnp.exp2(x_last) + _dot_tn(vb, k_st)
    return o, st_new


def _hgrn_body(qf_ref, ff_ref, vf_ref, qb_ref, fb_ref, vb_ref, gc_ref, cum_ref, lv_ref, of_ref, ob_ref,
               st_ref, xb_ref):
    @pl.when(pl.program_id(2) == 0)
    def _():
        st_ref[...] = jnp.zeros_like(st_ref)

    cs = SCAN_CHUNK
    n_chunks = SCAN_TILE // cs
    dirs = ((qf_ref, ff_ref, vf_ref, of_ref, False), (qb_ref, fb_ref, vb_ref, ob_ref, True))

    def step(j, carry):
        for d, (q_ref, f_ref, v_ref, o_ref, rev) in enumerate(dirs):
            jj = n_chunks - 1 - j if rev else j
            rows = pl.ds(pl.multiple_of(jj * cs, cs), cs)
            for hd in range(SCAN_HEADS):
                cols = slice(hd * A_HEAD_DIM, (hd + 1) * A_HEAD_DIM)
                o, st = _hgrn_chunk(q_ref[rows, cols], f_ref[rows, cols], v_ref[rows, cols], gc_ref[d][:, cols],
                                    st_ref[d, hd], cum_ref[d], lv_ref[d], rev, xb_ref.at[d, hd])
                o_ref[rows, cols] = o
                st_ref[d, hd] = st
        return carry

    lax.fori_loop(0, n_chunks, step, 0)


def _hgrn(z, gate_consts, batch, seq):
    nt = seq // SCAN_TILE
    width = SCAN_HEADS * A_HEAD_DIM
    hw = A_HEADS // SCAN_HEADS
    cum_tab, lv_tab = _scan_tables()

    def fwd(group):
        return pl.BlockSpec((SCAN_TILE, width), lambda b, h, i: (b * nt + i, group * hw + h))

    def bwd(group):
        return pl.BlockSpec((SCAN_TILE, width), lambda b, h, i: (b * nt + nt - 1 - i, group * hw + h))

    out = jax.ShapeDtypeStruct((batch * seq, A_WIDTH), F32)
    state = (2, SCAN_HEADS, A_HEAD_DIM, A_HEAD_DIM)
    return pl.pallas_call(
        _hgrn_body,
        grid=(batch, hw, nt),
        in_specs=[fwd(0), fwd(1), fwd(3), bwd(0), bwd(2), bwd(3),
                  pl.BlockSpec((2, 2, width), lambda b, h, i: (0, 0, h)),
                  _resident(cum_tab.shape), _resident(lv_tab.shape)],
        out_specs=[pl.BlockSpec((SCAN_TILE, width), lambda b, h, i: (b * nt + i, h)),
                   pl.BlockSpec((SCAN_TILE, width), lambda b, h, i: (b * nt + nt - 1 - i, h))],
        out_shape=[out, out],
        scratch_shapes=[pltpu.VMEM(state, F32), pltpu.VMEM((2, SCAN_HEADS, SCAN_CHUNK, A_HEAD_DIM), F32)],
        compiler_params=_params("parallel", "parallel", "arbitrary"),
        name="hgrn2_scan",
    )(z, z, z, z, z, z, gate_consts, cum_tab, lv_tab)


HALF = LANE // 2


def _attn_bias(kvh, second):
    blk = ATTN_BLOCK
    t = lax.broadcasted_iota(jnp.int32, (2 * blk, 3 * blk), 0)
    s = lax.broadcasted_iota(jnp.int32, (2 * blk, 3 * blk), 1)
    upper = t >= blk
    dist = jnp.abs(jnp.where(upper, t - blk, t) - s + blk)
    head_lo = kvh * B_GROUP + second
    slope_lo = LOG2E * 2.0 ** (-(8.0 / B_HEADS) * (head_lo + 1))
    slope_hi = LOG2E * 2.0 ** (-(8.0 / B_HEADS) * (head_lo + 3))
    slope = jnp.where(upper, slope_hi, slope_lo)
    return jnp.where(dist <= WINDOW, -slope * dist.astype(F32), -jnp.inf)


def _attn_body(sink_ref, q_ref, kc_ref, kp_ref, kn_ref, vc_ref, vp_ref, vn_ref, o_ref, kv_scr, bias_scr):
    i = pl.program_id(1)
    blk = ATTN_BLOCK

    @pl.when((pl.program_id(0) == 0) & (i == 0))
    def _():
        for kvh in range(B_KV_HEADS):
            for second in range(2):
                bias_scr[kvh, second] = _attn_bias(kvh, second)

    lane = lax.broadcasted_iota(jnp.int32, (ATTN_TILE + 2 * blk, LANE), 1)
    low = lane < HALF
    for x, (p_ref, c_ref, n_ref) in enumerate(((kp_ref, kc_ref, kn_ref), (vp_ref, vc_ref, vn_ref))):
        full = jnp.concatenate([p_ref[...], c_ref[...], n_ref[...]], axis=0)
        swapped = pltpu.roll(full, HALF, 1)
        kv_scr[x, 0, 0] = jnp.where(low, full, 0.0).astype(BF16)
        kv_scr[x, 0, 1] = jnp.where(low, 0.0, swapped).astype(BF16)
        kv_scr[x, 1, 0] = jnp.where(low, swapped, 0.0).astype(BF16)
        kv_scr[x, 1, 1] = jnp.where(low, 0.0, full).astype(BF16)

    s_idx = lax.broadcasted_iota(jnp.int32, (2 * blk, 3 * blk), 1)
    key_lo = jnp.where(i == 0, blk, 0)
    key_hi = jnp.where(i == pl.num_programs(1) - 1, 2 * blk, 3 * blk)
    upper = lax.broadcasted_iota(jnp.int32, (2 * blk, 1), 0) >= blk
    out_low = lax.broadcasted_iota(jnp.int32, (2 * blk, LANE), 1) < HALF
    n_blocks = ATTN_TILE // blk
    q_scale = LOG2E * B_HEAD_DIM ** -0.5
    for j in range(n_blocks):
        rows = slice(j * blk, (j + 1) * blk)
        keys = slice(j * blk, (j + 3) * blk)
        for kvh in range(B_KV_HEADS):
            c0 = kvh * B_GROUP * B_HEAD_DIM
            q2 = jnp.concatenate([q_ref[rows, c0:c0 + LANE], q_ref[rows, c0 + LANE:c0 + 2 * LANE]], axis=0)
            q2 = (q2 * q_scale).astype(BF16)
            probs, denoms = [], []
            for second in range(2):
                sc = _dot_nt(q2, kv_scr[0, kvh, second, keys, :]) + bias_scr[kvh, second]
                if j == 0:
                    sc = jnp.where(s_idx >= key_lo, sc, -jnp.inf)
                if j == n_blocks - 1:
                    sc = jnp.where(s_idx < key_hi, sc, -jnp.inf)
                head = kvh * B_GROUP + second
                sink = LOG2E * jnp.where(upper, sink_ref[head + 2], sink_ref[head])
                m = jnp.maximum(jnp.max(sc, axis=-1, keepdims=True), sink)
                p = jnp.exp2(sc - m)
                denoms.append(jnp.sum(p, axis=-1, keepdims=True) + jnp.exp2(sink - m))
                probs.append(p.astype(BF16))
            o = _dot(probs[0], kv_scr[1, kvh, 0, keys, :]) + _dot(probs[1], kv_scr[1, kvh, 1, keys, :])
            o = o / jnp.where(out_low, denoms[0], denoms[1])
            o_ref[rows, c0:c0 + LANE] = o[0:blk]
            o_ref[rows, c0 + LANE:c0 + 2 * LANE] = o[blk:]


def _attn(z, sink, batch, seq):
    nt = seq // ATTN_TILE
    per = ATTN_TILE // ATTN_BLOCK
    nblk = seq // ATTN_BLOCK
    q_col = 5 * A_WIDTH // B_WIDTH
    k_col = (5 * A_WIDTH + B_WIDTH) // B_KV_WIDTH
    v_col = k_col + 1

    def cur(col):
        return pl.BlockSpec((ATTN_TILE, B_KV_WIDTH), lambda b, i: (b * nt + i, col))

    def prev(col):
        return pl.BlockSpec((ATTN_BLOCK, B_KV_WIDTH),
                            lambda b, i: (b * nblk + jnp.maximum(i * per - 1, 0), col))

    def nxt(col):
        return pl.BlockSpec((ATTN_BLOCK, B_KV_WIDTH),
                            lambda b, i: (b * nblk + jnp.minimum((i + 1) * per, nblk - 1), col))

    return pl.pallas_call(
        _attn_body,
        grid=(batch, nt),
        in_specs=[pl.BlockSpec(memory_space=pltpu.SMEM),
                  pl.BlockSpec((ATTN_TILE, B_WIDTH), lambda b, i: (b * nt + i, q_col)),
                  cur(k_col), prev(k_col), nxt(k_col), cur(v_col), prev(v_col), nxt(v_col)],
        out_specs=pl.BlockSpec((ATTN_TILE, B_WIDTH), lambda b, i: (b * nt + i, 0)),
        out_shape=jax.ShapeDtypeStruct((batch * seq, B_WIDTH), F32),
        scratch_shapes=[pltpu.VMEM((2, B_KV_HEADS, 2, ATTN_TILE + 2 * ATTN_BLOCK, LANE), BF16),
                        pltpu.VMEM((B_KV_HEADS, 2, 2 * ATTN_BLOCK, 3 * ATTN_BLOCK), F32)],
        compiler_params=_params("arbitrary", "arbitrary"),
        name="window_attn",
    )(sink, z, z, z, z, z, z, z)


def _outproj_body(x_ref, of_ref, ob_ref, g_ref, oattn_ref, ng_ref, w_ref, lg_ref, lb_ref, y_ref):
    o = of_ref[...] + ob_ref[...]
    parts = []
    for h in range(A_HEADS):
        sl = slice(h * A_HEAD_DIM, (h + 1) * A_HEAD_DIM)
        oh = o[:, sl]
        oh = oh * lax.rsqrt(jnp.mean(oh * oh, axis=-1, keepdims=True) + RMS_EPS) * ng_ref[...]
        gh = g_ref[:, sl]
        parts.append((oh * (gh * jax.nn.sigmoid(gh))).astype(BF16))
    o_a = jnp.concatenate(parts, axis=-1)
    y = _dot(o_a, w_ref[0:A_WIDTH, :]) + _dot(oattn_ref[...].astype(BF16), w_ref[A_WIDTH:, :])
    y_ref[...] = _layernorm(DEEPNORM_ALPHA * x_ref[...] + y, lg_ref[...], lb_ref[...])


def _outproj(x, o_f, o_b, z, o_attn, norm_g, w, ln_g, ln_b):
    n = x.shape[0]
    tm = TOKEN_TILE
    g_col = 4 * A_WIDTH // A_WIDTH
    return pl.pallas_call(
        _outproj_body,
        grid=(n // tm,),
        in_specs=[pl.BlockSpec((tm, D_MODEL), lambda i: (i, 0)),
                  pl.BlockSpec((tm, A_WIDTH), lambda i: (i, 0)),
                  pl.BlockSpec((tm, A_WIDTH), lambda i: (i, 0)),
                  pl.BlockSpec((tm, A_WIDTH), lambda i: (i, g_col)),
                  pl.BlockSpec((tm, B_WIDTH), lambda i: (i, 0)),
                  _resident((1, A_HEAD_DIM)), _resident((A_WIDTH + B_WIDTH, D_MODEL)),
                  _resident((1, D_MODEL)), _resident((1, D_MODEL))],
        out_specs=pl.BlockSpec((tm, D_MODEL), lambda i: (i, 0)),
        out_shape=jax.ShapeDtypeStruct((n, D_MODEL), F32),
        compiler_params=_params("parallel"),
        name="outproj_ln",
    )(x, o_f, o_b, z, o_attn, norm_g, w, ln_g, ln_b)


def _sgu_body(x_ref, win_ref, cg_ref, cb_ref, ws_ref, bs_ref, wout_ref, lg_ref, lb_ref, y_ref,
              u_scr, v_scr, h_scr):
    tm = TOKEN_TILE
    xb = x_ref[...].astype(BF16)
    step = 2 * MXU_DIM
    for c in range(2 * C_WIDTH // step):
        z = _dot(xb, win_ref[:, c * step:(c + 1) * step])
        z = 0.5 * z * (1.0 + lax.erf(z * (2.0 ** -0.5)))
        if c < C_WIDTH // step:
            u_scr[:, c * step:(c + 1) * step] = z
        else:
            v_scr[:, c * step - C_WIDTH:(c + 1) * step - C_WIDTH] = z
    v = _layernorm(v_scr[...], cg_ref[...], cb_ref[...])
    v_scr[...] = v
    for g in range(C_GROUPS):
        w_g = ws_ref[g]
        bias = bs_ref[:, g:g + 1]
        cols = slice(g * C_GROUP_DIM, (g + 1) * C_GROUP_DIM)
        for n in range(tm // C_CHUNK):
            rows = slice(n * C_CHUNK, (n + 1) * C_CHUNK)
            s = _dot(w_g, v_scr[rows, cols].astype(BF16)) + bias
            h_scr[rows, cols] = (u_scr[rows, cols] * s).astype(BF16)
    y = _dot(h_scr[...], wout_ref[...])
    y_ref[...] = _layernorm(DEEPNORM_ALPHA * x_ref[...] + y, lg_ref[...], lb_ref[...])


def _sgu(x, w_in, c_g, c_b, w_s, b_s_t, w_out, ln_g, ln_b):
    n = x.shape[0]
    tm = TOKEN_TILE
    return pl.pallas_call(
        _sgu_body,
        grid=(n // tm,),
        in_specs=[pl.BlockSpec((tm, D_MODEL), lambda i: (i, 0)),
                  _resident((D_MODEL, 2 * C_WIDTH)), _resident((1, C_WIDTH)), _resident((1, C_WIDTH)),
                  _resident((C_GROUPS, C_CHUNK, C_CHUNK)), _resident((C_CHUNK, C_GROUPS)),
                  _resident((C_WIDTH, D_MODEL)), _resident((1, D_MODEL)), _resident((1, D_MODEL))],
        out_specs=pl.BlockSpec((tm, D_MODEL), lambda i: (i, 0)),
        out_shape=jax.ShapeDtypeStruct((n, D_MODEL), F32),
        scratch_shapes=[pltpu.VMEM((tm, C_WIDTH), F32), pltpu.VMEM((tm, C_WIDTH), F32),
                        pltpu.VMEM((tm, C_WIDTH), BF16)],
        compiler_params=_params("parallel"),
        name="sgu_ln",
    )(x, w_in, c_g, c_b, w_s, b_s_t, w_out, ln_g, ln_b)


def _ffn_body(x_ref, wg_ref, wu_ref, wd_ref, lg_ref, lb_ref, y_ref, h_scr):
    xb = x_ref[...].astype(BF16)
    for c in range(FFN_DIM // MXU_DIM):
        sl = slice(c * MXU_DIM, (c + 1) * MXU_DIM)
        gate = _dot(xb, wg_ref[:, sl])
        up = _dot(xb, wu_ref[:, sl])
        h_scr[:, sl] = (gate * jax.nn.sigmoid(gate) * up).astype(BF16)
    y = _dot(h_scr[...], wd_ref[...])
    y_ref[...] = _layernorm(DEEPNORM_ALPHA * x_ref[...] + y, lg_ref[...], lb_ref[...])


def _ffn(x, wg, wu, wd, ln_g, ln_b):
    n = x.shape[0]
    tm = TOKEN_TILE
    return pl.pallas_call(
        _ffn_body,
        grid=(n // tm,),
        in_specs=[pl.BlockSpec((tm, D_MODEL), lambda i: (i, 0)),
                  _resident((D_MODEL, FFN_DIM)), _resident((D_MODEL, FFN_DIM)), _resident((FFN_DIM, D_MODEL)),
                  _resident((1, D_MODEL)), _resident((1, D_MODEL))],
        out_specs=pl.BlockSpec((tm, D_MODEL), lambda i: (i, 0)),
        out_shape=jax.ShapeDtypeStruct((n, D_MODEL), F32),
        scratch_shapes=[pltpu.VMEM((tm, FFN_DIM), BF16)],
        compiler_params=_params("parallel"),
        name="swiglu_ln",
    )(x, wg, wu, wd, ln_g, ln_b)


def _trunk(x, p):
    batch, seq, _ = x.shape
    x = x.reshape(batch * seq, D_MODEL)
    for layer in range(DEPTH):
        j = layer // 2
        if layer % 2 == 0:
            z = _inproj(x, p["w_in_ab"][j])
            o_f, o_b = _hgrn(z, p["gate_consts"][j], batch, seq)
            o_attn = _attn(z, p["attn_sink"][j], batch, seq)
            x = _outproj(x, o_f, o_b, z, o_attn, p["hgrn_norm_g"][j], p["w_out_ab"][j],
                         p["ln_mix_g"][layer], p["ln_mix_b"][layer])
        else:
            x = _sgu(x, p["w_in_c"][j], p["c_ln_g"][j], p["c_ln_b"][j], p["c_ws"][j], p["c_bs_t"][j],
                     p["w_out_c"][j], p["ln_mix_g"][layer], p["ln_mix_b"][layer])
        x = _ffn(x, p["ffn_w_gate"][layer], p["ffn_w_up"][layer], p["ffn_w_down"][layer],
                 p["ln_ffn_g"][layer], p["ln_ffn_b"][layer])
    return x.reshape(batch, seq, D_MODEL)


def kernel(x_prompt, x_sample, w_in_ab, hgrn_lb_logits, hgrn_norm_g, attn_sink, w_out_ab, w_in_c, c_ln_g,
           c_ln_b, c_ws, c_bs, w_out_c, ffn_w_gate, ffn_w_up, ffn_w_down, ln_mix_g, ln_mix_b, ln_ffn_g,
           ln_ffn_b):
    prob = jax.nn.softmax(hgrn_lb_logits.astype(F32), axis=0)
    lower = jnp.maximum(jnp.cumsum(prob, axis=0) - prob[0:1], 0.0)
    gate_consts = jnp.stack([jnp.log(lower), jnp.log1p(-lower)], axis=2)
    row = lambda a: a.astype(F32)[:, None, :]
    p = dict(
        w_in_ab=w_in_ab.astype(BF16), gate_consts=gate_consts, hgrn_norm_g=row(hgrn_norm_g),
        attn_sink=attn_sink.astype(F32), w_out_ab=w_out_ab.astype(BF16),
        w_in_c=w_in_c.astype(BF16), c_ln_g=row(c_ln_g), c_ln_b=row(c_ln_b), c_ws=c_ws.astype(BF16),
        c_bs_t=jnp.swapaxes(c_bs.astype(F32), 1, 2), w_out_c=w_out_c.astype(BF16),
        ffn_w_gate=ffn_w_gate.astype(BF16), ffn_w_up=ffn_w_up.astype(BF16), ffn_w_down=ffn_w_down.astype(BF16),
        ln_mix_g=row(ln_mix_g), ln_mix_b=row(ln_mix_b), ln_ffn_g=row(ln_ffn_g), ln_ffn_b=row(ln_ffn_b),
    )
    return (_trunk(x_prompt, p), _trunk(x_sample, p))
```

```python
import functools
import math

import numpy as np
import jax
import jax.numpy as jnp
from jax import lax
from jax.experimental import pallas as pl
from jax.experimental.pallas import tpu as pltpu

D_MODEL = 1024
DEPTH = 4
A_HEADS = 4
A_HEAD_DIM = 128
A_WIDTH = A_HEADS * A_HEAD_DIM
B_HEADS = 8
B_KV_HEADS = 2
B_GROUP = B_HEADS // B_KV_HEADS
B_HEAD_DIM = 64
B_WIDTH = B_HEADS * B_HEAD_DIM
B_KV_WIDTH = B_KV_HEADS * B_HEAD_DIM
WINDOW = 128
ATTN_BLOCK = 128
AB_IN = 5 * A_WIDTH + B_WIDTH + 2 * B_KV_WIDTH
C_WIDTH = 2 * D_MODEL
C_GROUPS = 8
C_GROUP_DIM = C_WIDTH // C_GROUPS
C_CHUNK = 128
FFN_DIM = 2816
DEEPNORM_ALPHA = (2.0 * DEPTH) ** 0.25
LN_EPS = 1e-5
RMS_EPS = 1e-6

LANE = 128
MXU_DIM = 256
VMEM_LIMIT = 56 * 1024 * 1024

TOKEN_TILE = 512
SCAN_TILE = 1024
SCAN_CHUNK = 128
ATTN_TILE = 512

F32 = jnp.float32
BF16 = jnp.bfloat16
LOG2E = math.log2(math.e)


def _dot(a, b):
    return jnp.dot(a, b, preferred_element_type=F32)


def _dot_nt(a, b):
    return lax.dot_general(a, b, (((1,), (1,)), ((), ())), preferred_element_type=F32)


def _dot_tn(a, b):
    return lax.dot_general(a, b, (((0,), (0,)), ((), ())), preferred_element_type=F32)


def _layernorm(x, g, b):
    mu = jnp.mean(x, axis=-1, keepdims=True)
    xc = x - mu
    var = jnp.mean(xc * xc, axis=-1, keepdims=True)
    return xc * lax.rsqrt(var + LN_EPS) * g + b


def _resident(shape):
    nd = len(shape)
    return pl.BlockSpec(shape, lambda *_: (0,) * nd, pipeline_mode=pl.Buffered(1))


def _params(*sem):
    return pltpu.CompilerParams(dimension_semantics=sem, vmem_limit_bytes=VMEM_LIMIT)


ZB_Q, ZB_H1F, ZB_H2F, ZB_KF, ZB_H1B, ZB_H2B, ZB_KB, ZB_V = range(8)
ZB_WIDTH = 8 * A_WIDTH
ZF_G, ZF_QATTN, ZF_KV = 0, A_WIDTH, A_WIDTH + B_WIDTH
ZF_WIDTH = A_WIDTH + B_WIDTH + 2 * B_KV_WIDTH


def _forget_gate(x, log_lb, l1m):
    log_sig = jnp.minimum(x, 0.0) - jnp.log(1.0 + jnp.exp(-jnp.abs(x)))
    c = l1m + log_sig
    logf = jnp.maximum(log_lb, c) + jnp.log(1.0 + jnp.exp(-jnp.abs(log_lb - c)))
    lf2 = logf * LOG2E
    hi = lf2.astype(BF16)
    lo = (lf2 - hi.astype(F32)).astype(BF16)
    k = jnp.exp(c - x)
    return hi, lo, k.astype(BF16)


def _inproj_body(x_ref, w_ref, gc_ref, zb_ref, zf_ref):
    step = MXU_DIM
    per_group = A_WIDTH // step
    n_chunks = AB_IN // step
    gate_chunks = list(range(per_group, 3 * per_group))
    other_chunks = [c for c in range(n_chunks) if c not in gate_chunks]
    order = []
    while gate_chunks or other_chunks:
        order += gate_chunks[:1] + other_chunks[:2]
        gate_chunks, other_chunks = gate_chunks[1:], other_chunks[2:]
    xb = x_ref[...].astype(BF16)
    for c in order:
        z = _dot(xb, w_ref[:, c * step:(c + 1) * step])
        group, part = divmod(c, per_group)
        sub = slice(part * step, (part + 1) * step)

        def put(zb_group, val):
            zb_ref[:, zb_group * A_WIDTH + part * step:zb_group * A_WIDTH + (part + 1) * step] = val

        if group == 0:
            put(ZB_Q, (z * jax.nn.sigmoid(z)).astype(BF16))
        elif group in (1, 2):
            d = group - 1
            hi, lo, k = _forget_gate(z, gc_ref[d, 0:1, sub], gc_ref[d, 1:2, sub])
            put(ZB_H1F if d == 0 else ZB_H1B, hi)
            put(ZB_H2F if d == 0 else ZB_H2B, lo)
            put(ZB_KF if d == 0 else ZB_KB, k)
        elif group == 3:
            put(ZB_V, z.astype(BF16))
        else:
            off = c * step - 4 * A_WIDTH
            zf_ref[:, off:off + step] = z


def _inproj(x, w, gate_consts):
    n = x.shape[0]
    tm = TOKEN_TILE
    return pl.pallas_call(
        _inproj_body,
        grid=(n // tm,),
        in_specs=[pl.BlockSpec((tm, D_MODEL), lambda i: (i, 0)), _resident((D_MODEL, AB_IN)),
                  _resident((2, 2, A_WIDTH))],
        out_specs=[pl.BlockSpec((tm, ZB_WIDTH), lambda i: (i, 0)), pl.BlockSpec((tm, ZF_WIDTH), lambda i: (i, 0))],
        out_shape=[jax.ShapeDtypeStruct((n, ZB_WIDTH), BF16), jax.ShapeDtypeStruct((n, ZF_WIDTH), F32)],
        compiler_params=_params("parallel"),
        name="inproj",
    )(x, w, gate_consts)


N_LEVELS = SCAN_CHUNK.bit_length() - 1
SUBLANES = 8
SCAN_HEADS = 2
MXU_LEVELS = 2


def _level_matrix(lvl):
    c = SCAN_CHUNK
    h = c >> lvl
    t = np.arange(c)[:, None]
    u = np.arange(c)[None, :]
    boundary = (t // (2 * h)) * 2 * h + h - 1
    second = (t % (2 * h)) >= h
    return np.where(second, (u > boundary) & (u <= t), (u > t) & (u <= boundary))


def _scan_tables():
    c = SCAN_CHUNK
    t = np.arange(c)[:, None]
    u = np.arange(c)[None, :]
    level = np.where(u == t, 0, -1)
    for lvl in range(1, N_LEVELS + 1):
        h = c >> lvl
        level = np.where((u < t) & ((t ^ u) >= h) & ((t ^ u) < 2 * h), lvl, level)
    mats = [u <= t] + [_level_matrix(lvl) for lvl in range(N_LEVELS - MXU_LEVELS + 1, N_LEVELS + 1)]
    fwd = np.concatenate(mats, axis=0)
    bwd = np.concatenate([m[::-1, ::-1] for m in mats], axis=0)
    rs = np.stack([fwd, bwd]).astype(np.float32)
    rs2 = jnp.asarray(np.concatenate([rs, rs], axis=2), dtype=BF16)
    lv = jnp.asarray(np.stack([level, level.T]).astype(np.int32))
    return rs2, lv


def _neg_abs(x):
    bits = lax.bitcast_convert_type(x, jnp.uint32) | jnp.uint32(0x80000000)
    return lax.bitcast_convert_type(bits, F32)


def _level_exponents(xb, xb_ref, lvl, rev):
    cs = SCAN_CHUNK
    h = cs >> lvl
    assert 2 * h >= SUBLANES
    pieces = []
    for start in range(0, cs, 2 * h):
        bnd = start + (h if rev else h - 1)
        pieces.append(_neg_abs(xb[start:start + 2 * h] - xb_ref[bnd:bnd + 1, :]))
    return jnp.concatenate(pieces, axis=0)


def _hgrn_chunks(q, k, v, hilo, sts, rs2, level, xb_ref):
    cs = SCAN_CHUNK
    dirs = (0, 1)
    heads = [slice(hd * A_HEAD_DIM, (hd + 1) * A_HEAD_DIM) for hd in range(SCAN_HEADS)]
    x = [_dot(rs2[d], hilo[d]) for d in dirs]
    xb = [x[d][0:cs] for d in dirs]
    for d in dirs:
        xb_ref[d] = xb[d]
    x_last = [xb_ref[d, (0 if d else cs - 1):(1 if d else cs), :] for d in dirs]

    k_t = [[k[d][:, hs].T for hs in heads] for d in dirs]
    a = [[jnp.where(level[d] == 0, _dot(q[d][:, hs], kt), 0.0) for hs, kt in zip(heads, k_t[d])] for d in dirs]
    for lvl in range(1, N_LEVELS + 1):
        from_mxu = lvl - (N_LEVELS - MXU_LEVELS)
        for d in dirs:
            if from_mxu >= 1:
                e = x[d][from_mxu * cs:(from_mxu + 1) * cs]
            else:
                e = _level_exponents(xb[d], xb_ref.at[d], lvl, bool(d))
            p = jnp.exp2(e).astype(BF16)
            qp = q[d] * p
            a[d] = [jnp.where(level[d] == lvl, _dot(qp[:, hs], kt * p[:, hs].T), a_h)
                    for hs, kt, a_h in zip(heads, k_t[d], a[d])]

    outs, new_sts = [], []
    for d in dirs:
        q_st = q[d] * jnp.exp2(xb[d]).astype(BF16)
        k_st = k[d] * jnp.exp2(_neg_abs(xb[d] - x_last[d])).astype(BF16)
        decay = jnp.exp2(x_last[d])
        outs.append([_dot(a_h.astype(BF16), v[d][:, hs]) + _dot_nt(q_st[:, hs], st.astype(BF16))
                     for hs, a_h, st in zip(heads, a[d], sts[d])])
        new_sts.append([st * decay[:, hs] + _dot_tn(v[d][:, hs], k_st[:, hs]) for hs, st in zip(heads, sts[d])])
    return outs, new_sts


def _hgrn_body(qf_ref, h1f_ref, h2f_ref, kf_ref, vf_ref, qb_ref, h1b_ref, h2b_ref, kb_ref, vb_ref,
               rs_ref, lv_ref, of_ref, ob_ref, st_ref, xb_ref):
    @pl.when(pl.program_id(2) == 0)
    def _():
        st_ref[...] = jnp.zeros_like(st_ref)

    cs = SCAN_CHUNK
    n_chunks = SCAN_TILE // cs
    dirs = (0, 1)
    q_refs, h1_refs, h2_refs = (qf_ref, qb_ref), (h1f_ref, h1b_ref), (h2f_ref, h2b_ref)
    k_refs, v_refs, o_refs = (kf_ref, kb_ref), (vf_ref, vb_ref), (of_ref, ob_ref)

    def step(j, carry):
        rows = [pl.ds(pl.multiple_of((n_chunks - 1 - j if d else j) * cs, cs), cs) for d in dirs]
        hilo = [jnp.concatenate([h1_refs[d][rows[d], :], h2_refs[d][rows[d], :]], axis=0) for d in dirs]
        outs, sts = _hgrn_chunks([q_refs[d][rows[d], :] for d in dirs], [k_refs[d][rows[d], :] for d in dirs],
                                 [v_refs[d][rows[d], :] for d in dirs], hilo,
                                 [[st_ref[d, hd] for hd in range(SCAN_HEADS)] for d in dirs],
                                 [rs_ref[d] for d in dirs], [lv_ref[d] for d in dirs], xb_ref)
        for d in dirs:
            for hd in range(SCAN_HEADS):
                o_refs[d][rows[d], hd * A_HEAD_DIM:(hd + 1) * A_HEAD_DIM] = outs[d][hd]
                st_ref[d, hd] = sts[d][hd]
        return carry

    lax.fori_loop(0, n_chunks, step, 0, unroll=2)


def _hgrn(zb, batch, seq):
    nt = seq // SCAN_TILE
    width = SCAN_HEADS * A_HEAD_DIM
    hw = A_HEADS // SCAN_HEADS
    rs_tab, lv_tab = _scan_tables()

    def fwd(group):
        return pl.BlockSpec((SCAN_TILE, width), lambda b, h, i: (b * nt + i, group * hw + h))

    def bwd(group):
        return pl.BlockSpec((SCAN_TILE, width), lambda b, h, i: (b * nt + nt - 1 - i, group * hw + h))

    out = jax.ShapeDtypeStruct((batch * seq, A_WIDTH), F32)
    state = (2, SCAN_HEADS, A_HEAD_DIM, A_HEAD_DIM)
    return pl.pallas_call(
        _hgrn_body,
        grid=(batch, hw, nt),
        in_specs=[fwd(ZB_Q), fwd(ZB_H1F), fwd(ZB_H2F), fwd(ZB_KF), fwd(ZB_V),
                  bwd(ZB_Q), bwd(ZB_H1B), bwd(ZB_H2B), bwd(ZB_KB), bwd(ZB_V),
                  _resident(rs_tab.shape), _resident(lv_tab.shape)],
        out_specs=[pl.BlockSpec((SCAN_TILE, width), lambda b, h, i: (b * nt + i, h)),
                   pl.BlockSpec((SCAN_TILE, width), lambda b, h, i: (b * nt + nt - 1 - i, h))],
        out_shape=[out, out],
        scratch_shapes=[pltpu.VMEM(state, F32), pltpu.VMEM((2, SCAN_CHUNK, width), F32)],
        compiler_params=_params("parallel", "parallel", "arbitrary"),
        name="hgrn2_scan",
    )(*([zb] * 10), rs_tab, lv_tab)


HALF = LANE // 2


def _attn_bias(kvh, second):
    blk = ATTN_BLOCK
    t = lax.broadcasted_iota(jnp.int32, (2 * blk, 3 * blk), 0)
    s = lax.broadcasted_iota(jnp.int32, (2 * blk, 3 * blk), 1)
    upper = t >= blk
    dist = jnp.abs(jnp.where(upper, t - blk, t) - s + blk)
    head_lo = kvh * B_GROUP + second
    slope_lo = LOG2E * 2.0 ** (-(8.0 / B_HEADS) * (head_lo + 1))
    slope_hi = LOG2E * 2.0 ** (-(8.0 / B_HEADS) * (head_lo + 3))
    slope = jnp.where(upper, slope_hi, slope_lo)
    return jnp.where(dist <= WINDOW, -slope * dist.astype(F32), -jnp.inf)


def _attn_body(sink_ref, q_ref, kc_ref, kp_ref, kn_ref, vc_ref, vp_ref, vn_ref, o_ref, kv_scr, bias_scr):
    i = pl.program_id(1)
    blk = ATTN_BLOCK

    @pl.when((pl.program_id(0) == 0) & (i == 0))
    def _():
        for kvh in range(B_KV_HEADS):
            for second in range(2):
                bias_scr[kvh, second] = _attn_bias(kvh, second)

    lane = lax.broadcasted_iota(jnp.int32, (ATTN_TILE + 2 * blk, LANE), 1)
    low = lane < HALF
    for x, (p_ref, c_ref, n_ref) in enumerate(((kp_ref, kc_ref, kn_ref), (vp_ref, vc_ref, vn_ref))):
        full = jnp.concatenate([p_ref[...], c_ref[...], n_ref[...]], axis=0)
        swapped = pltpu.roll(full, HALF, 1)
        kv_scr[x, 0, 0] = jnp.where(low, full, 0.0).astype(BF16)
        kv_scr[x, 0, 1] = jnp.where(low, 0.0, swapped).astype(BF16)
        kv_scr[x, 1, 0] = jnp.where(low, swapped, 0.0).astype(BF16)
        kv_scr[x, 1, 1] = jnp.where(low, 0.0, full).astype(BF16)

    s_idx = lax.broadcasted_iota(jnp.int32, (2 * blk, 3 * blk), 1)
    key_lo = jnp.where(i == 0, blk, 0)
    key_hi = jnp.where(i == pl.num_programs(1) - 1, 2 * blk, 3 * blk)
    upper = lax.broadcasted_iota(jnp.int32, (2 * blk, 1), 0) >= blk
    out_low = lax.broadcasted_iota(jnp.int32, (2 * blk, LANE), 1) < HALF
    n_blocks = ATTN_TILE // blk
    q_scale = LOG2E * B_HEAD_DIM ** -0.5
    for j in range(n_blocks):
        rows = slice(j * blk, (j + 1) * blk)
        keys = slice(j * blk, (j + 3) * blk)
        for kvh in range(B_KV_HEADS):
            c0 = kvh * B_GROUP * B_HEAD_DIM
            q2 = jnp.concatenate([q_ref[rows, c0:c0 + LANE], q_ref[rows, c0 + LANE:c0 + 2 * LANE]], axis=0)
            q2 = (q2 * q_scale).astype(BF16)
            probs, denoms = [], []
            for second in range(2):
                sc = _dot_nt(q2, kv_scr[0, kvh, second, keys, :]) + bias_scr[kvh, second]
                if j == 0:
                    sc = jnp.where(s_idx >= key_lo, sc, -jnp.inf)
                if j == n_blocks - 1:
                    sc = jnp.where(s_idx < key_hi, sc, -jnp.inf)
                head = kvh * B_GROUP + second
                sink = LOG2E * jnp.where(upper, sink_ref[head + 2], sink_ref[head])
                m = jnp.maximum(jnp.max(sc, axis=-1, keepdims=True), sink)
                p = jnp.exp2(sc - m)
                denoms.append(jnp.sum(p, axis=-1, keepdims=True) + jnp.exp2(sink - m))
                probs.append(p.astype(BF16))
            o = _dot(probs[0], kv_scr[1, kvh, 0, keys, :]) + _dot(probs[1], kv_scr[1, kvh, 1, keys, :])
            o = o / jnp.where(out_low, denoms[0], denoms[1])
            o_ref[rows, c0:c0 + LANE] = o[0:blk]
            o_ref[rows, c0 + LANE:c0 + 2 * LANE] = o[blk:]


def _attn(z, sink, batch, seq):
    nt = seq // ATTN_TILE
    per = ATTN_TILE // ATTN_BLOCK
    nblk = seq // ATTN_BLOCK
    q_col = ZF_QATTN // B_WIDTH
    k_col = ZF_KV // B_KV_WIDTH
    v_col = k_col + 1

    def cur(col):
        return pl.BlockSpec((ATTN_TILE, B_KV_WIDTH), lambda b, i: (b * nt + i, col))

    def prev(col):
        return pl.BlockSpec((ATTN_BLOCK, B_KV_WIDTH),
                            lambda b, i: (b * nblk + jnp.maximum(i * per - 1, 0), col))

    def nxt(col):
        return pl.BlockSpec((ATTN_BLOCK, B_KV_WIDTH),
                            lambda b, i: (b * nblk + jnp.minimum((i + 1) * per, nblk - 1), col))

    return pl.pallas_call(
        _attn_body,
        grid=(batch, nt),
        in_specs=[pl.BlockSpec(memory_space=pltpu.SMEM),
                  pl.BlockSpec((ATTN_TILE, B_WIDTH), lambda b, i: (b * nt + i, q_col)),
                  cur(k_col), prev(k_col), nxt(k_col), cur(v_col), prev(v_col), nxt(v_col)],
        out_specs=pl.BlockSpec((ATTN_TILE, B_WIDTH), lambda b, i: (b * nt + i, 0)),
        out_shape=jax.ShapeDtypeStruct((batch * seq, B_WIDTH), F32),
        scratch_shapes=[pltpu.VMEM((2, B_KV_HEADS, 2, ATTN_TILE + 2 * ATTN_BLOCK, LANE), BF16),
                        pltpu.VMEM((B_KV_HEADS, 2, 2 * ATTN_BLOCK, 3 * ATTN_BLOCK), F32)],
        compiler_params=_params("arbitrary", "arbitrary"),
        name="window_attn",
    )(sink, z, z, z, z, z, z, z)


def _outproj_body(x_ref, of_ref, ob_ref, g_ref, oattn_ref, ng_ref, w_ref, lg_ref, lb_ref, y_ref):
    o = of_ref[...] + ob_ref[...]
    parts = []
    for h in range(A_HEADS):
        sl = slice(h * A_HEAD_DIM, (h + 1) * A_HEAD_DIM)
        oh = o[:, sl]
        oh = oh * lax.rsqrt(jnp.mean(oh * oh, axis=-1, keepdims=True) + RMS_EPS) * ng_ref[...]
        gh = g_ref[:, sl]
        parts.append((oh * (gh * jax.nn.sigmoid(gh))).astype(BF16))
    o_a = jnp.concatenate(parts, axis=-1)
    y = _dot(o_a, w_ref[0:A_WIDTH, :]) + _dot(oattn_ref[...].astype(BF16), w_ref[A_WIDTH:, :])
    y_ref[...] = _layernorm(DEEPNORM_ALPHA * x_ref[...] + y, lg_ref[...], lb_ref[...])


def _outproj(x, o_f, o_b, z, o_attn, norm_g, w, ln_g, ln_b):
    n = x.shape[0]
    tm = TOKEN_TILE
    g_col = ZF_G // A_WIDTH
    return pl.pallas_call(
        _outproj_body,
        grid=(n // tm,),
        in_specs=[pl.BlockSpec((tm, D_MODEL), lambda i: (i, 0)),
                  pl.BlockSpec((tm, A_WIDTH), lambda i: (i, 0)),
                  pl.BlockSpec((tm, A_WIDTH), lambda i: (i, 0)),
                  pl.BlockSpec((tm, A_WIDTH), lambda i: (i, g_col)),
                  pl.BlockSpec((tm, B_WIDTH), lambda i: (i, 0)),
                  _resident((1, A_HEAD_DIM)), _resident((A_WIDTH + B_WIDTH, D_MODEL)),
                  _resident((1, D_MODEL)), _resident((1, D_MODEL))],
        out_specs=pl.BlockSpec((tm, D_MODEL), lambda i: (i, 0)),
        out_shape=jax.ShapeDtypeStruct((n, D_MODEL), F32),
        compiler_params=_params("parallel"),
        name="outproj_ln",
    )(x, o_f, o_b, z, o_attn, norm_g, w, ln_g, ln_b)


def _sgu_body(x_ref, win_ref, cg_ref, cb_ref, ws_ref, bs_ref, wout_ref, lg_ref, lb_ref, y_ref,
              u_scr, v_scr, h_scr):
    tm = TOKEN_TILE
    xb = x_ref[...].astype(BF16)
    step = 2 * MXU_DIM
    for c in range(2 * C_WIDTH // step):
        z = _dot(xb, win_ref[:, c * step:(c + 1) * step])
        z = 0.5 * z * (1.0 + lax.erf(z * (2.0 ** -0.5)))
        if c < C_WIDTH // step:
            u_scr[:, c * step:(c + 1) * step] = z
        else:
            v_scr[:, c * step - C_WIDTH:(c + 1) * step - C_WIDTH] = z
    v = _layernorm(v_scr[...], cg_ref[...], cb_ref[...])
    v_scr[...] = v
    for g in range(C_GROUPS):
        w_g = ws_ref[g]
        bias = bs_ref[:, g:g + 1]
        cols = slice(g * C_GROUP_DIM, (g + 1) * C_GROUP_DIM)
        for n in range(tm // C_CHUNK):
            rows = slice(n * C_CHUNK, (n + 1) * C_CHUNK)
            s = _dot(w_g, v_scr[rows, cols].astype(BF16)) + bias
            h_scr[rows, cols] = (u_scr[rows, cols] * s).astype(BF16)
    y = _dot(h_scr[...], wout_ref[...])
    y_ref[...] = _layernorm(DEEPNORM_ALPHA * x_ref[...] + y, lg_ref[...], lb_ref[...])


def _sgu(x, w_in, c_g, c_b, w_s, b_s_t, w_out, ln_g, ln_b):
    n = x.shape[0]
    tm = TOKEN_TILE
    return pl.pallas_call(
        _sgu_body,
        grid=(n // tm,),
        in_specs=[pl.BlockSpec((tm, D_MODEL), lambda i: (i, 0)),
                  _resident((D_MODEL, 2 * C_WIDTH)), _resident((1, C_WIDTH)), _resident((1, C_WIDTH)),
                  _resident((C_GROUPS, C_CHUNK, C_CHUNK)), _resident((C_CHUNK, C_GROUPS)),
                  _resident((C_WIDTH, D_MODEL)), _resident((1, D_MODEL)), _resident((1, D_MODEL))],
        out_specs=pl.BlockSpec((tm, D_MODEL), lambda i: (i, 0)),
        out_shape=jax.ShapeDtypeStruct((n, D_MODEL), F32),
        scratch_shapes=[pltpu.VMEM((tm, C_WIDTH), F32), pltpu.VMEM((tm, C_WIDTH), F32),
                        pltpu.VMEM((tm, C_WIDTH), BF16)],
        compiler_params=_params("parallel"),
        name="sgu_ln",
    )(x, w_in, c_g, c_b, w_s, b_s_t, w_out, ln_g, ln_b)


def _ffn_body(x_ref, wg_ref, wu_ref, wd_ref, lg_ref, lb_ref, y_ref, h_scr):
    xb = x_ref[...].astype(BF16)
    for c in range(FFN_DIM // MXU_DIM):
        sl = slice(c * MXU_DIM, (c + 1) * MXU_DIM)
        gate = _dot(xb, wg_ref[:, sl])
        up = _dot(xb, wu_ref[:, sl])
        h_scr[:, sl] = (gate * jax.nn.sigmoid(gate) * up).astype(BF16)
    y = _dot(h_scr[...], wd_ref[...])
    y_ref[...] = _layernorm(DEEPNORM_ALPHA * x_ref[...] + y, lg_ref[...], lb_ref[...])


def _ffn(x, wg, wu, wd, ln_g, ln_b):
    n = x.shape[0]
    tm = TOKEN_TILE
    return pl.pallas_call(
        _ffn_body,
        grid=(n // tm,),
        in_specs=[pl.BlockSpec((tm, D_MODEL), lambda i: (i, 0)),
                  _resident((D_MODEL, FFN_DIM)), _resident((D_MODEL, FFN_DIM)), _resident((FFN_DIM, D_MODEL)),
                  _resident((1, D_MODEL)), _resident((1, D_MODEL))],
        out_specs=pl.BlockSpec((tm, D_MODEL), lambda i: (i, 0)),
        out_shape=jax.ShapeDtypeStruct((n, D_MODEL), F32),
        scratch_shapes=[pltpu.VMEM((tm, FFN_DIM), BF16)],
        compiler_params=_params("parallel"),
        name="swiglu_ln",
    )(x, wg, wu, wd, ln_g, ln_b)


def _trunk(x, p):
    batch, seq, _ = x.shape
    x = x.reshape(batch * seq, D_MODEL)
    for layer in range(DEPTH):
        j = layer // 2
        if layer % 2 == 0:
            zb, zf = _inproj(x, p["w_in_ab"][j], p["gate_consts"][j])
            o_f, o_b = _hgrn(zb, batch, seq)
            o_attn = _attn(zf, p["attn_sink"][j], batch, seq)
            x = _outproj(x, o_f, o_b, zf, o_attn, p["hgrn_norm_g"][j], p["w_out_ab"][j],
                         p["ln_mix_g"][layer], p["ln_mix_b"][layer])
        else:
            x = _sgu(x, p["w_in_c"][j], p["c_ln_g"][j], p["c_ln_b"][j], p["c_ws"][j], p["c_bs_t"][j],
                     p["w_out_c"][j], p["ln_mix_g"][layer], p["ln_mix_b"][layer])
        x = _ffn(x, p["ffn_w_gate"][layer], p["ffn_w_up"][layer], p["ffn_w_down"][layer],
                 p["ln_ffn_g"][layer], p["ln_ffn_b"][layer])
    return x.reshape(batch, seq, D_MODEL)


def kernel(x_prompt, x_sample, w_in_ab, hgrn_lb_logits, hgrn_norm_g, attn_sink, w_out_ab, w_in_c, c_ln_g,
           c_ln_b, c_ws, c_bs, w_out_c, ffn_w_gate, ffn_w_up, ffn_w_down, ln_mix_g, ln_mix_b, ln_ffn_g,
           ln_ffn_b):
    prob = jax.nn.softmax(hgrn_lb_logits.astype(F32), axis=0)
    lower = jnp.maximum(jnp.cumsum(prob, axis=0) - prob[0:1], 0.0)
    gate_consts = jnp.stack([jnp.log(lower), jnp.log1p(-lower)], axis=2)
    row = lambda a: a.astype(F32)[:, None, :]
    p = dict(
        w_in_ab=w_in_ab.astype(BF16), gate_consts=gate_consts, hgrn_norm_g=row(hgrn_norm_g),
        attn_sink=attn_sink.astype(F32), w_out_ab=w_out_ab.astype(BF16),
        w_in_c=w_in_c.astype(BF16), c_ln_g=row(c_ln_g), c_ln_b=row(c_ln_b), c_ws=c_ws.astype(BF16),
        c_bs_t=jnp.swapaxes(c_bs.astype(F32), 1, 2), w_out_c=w_out_c.astype(BF16),
        ffn_w_gate=ffn_w_gate.astype(BF16), ffn_w_up=ffn_w_up.astype(BF16), ffn_w_down=ffn_w_down.astype(BF16),
        ln_mix_g=row(ln_mix_g), ln_mix_b=row(ln_mix_b), ln_ffn_g=row(ln_ffn_g), ln_ffn_b=row(ln_ffn_b),
    )
    return (_trunk(x_prompt, p), _trunk(x_sample, p))
```

```python
import functools
import math

import numpy as np
import jax
import jax.numpy as jnp
from jax import lax
from jax.experimental import pallas as pl
from jax.experimental.pallas import tpu as pltpu

D_MODEL = 1024
DEPTH = 4
A_HEADS = 4
A_HEAD_DIM = 128
A_WIDTH = A_HEADS * A_HEAD_DIM
B_HEADS = 8
B_KV_HEADS = 2
B_GROUP = B_HEADS // B_KV_HEADS
B_HEAD_DIM = 64
B_WIDTH = B_HEADS * B_HEAD_DIM
B_KV_WIDTH = B_KV_HEADS * B_HEAD_DIM
WINDOW = 128
ATTN_BLOCK = 128
AB_IN = 5 * A_WIDTH + B_WIDTH + 2 * B_KV_WIDTH
C_WIDTH = 2 * D_MODEL
C_GROUPS = 8
C_GROUP_DIM = C_WIDTH // C_GROUPS
C_CHUNK = 128
FFN_DIM = 2816
DEEPNORM_ALPHA = (2.0 * DEPTH) ** 0.25
LN_EPS = 1e-5
RMS_EPS = 1e-6

LANE = 128
MXU_DIM = 256
VMEM_LIMIT = 56 * 1024 * 1024

TOKEN_TILE = 512
FFN_TILE = 1024
ROW_BLOCK = 256
SCAN_TILE = 1024
SCAN_CHUNK = 128
ATTN_TILE = 512

F32 = jnp.float32
BF16 = jnp.bfloat16
LOG2E = math.log2(math.e)


def _dot(a, b):
    return jnp.dot(a, b, preferred_element_type=F32)


def _dot_nt(a, b):
    return lax.dot_general(a, b, (((1,), (1,)), ((), ())), preferred_element_type=F32)


def _dot_tn(a, b):
    return lax.dot_general(a, b, (((0,), (0,)), ((), ())), preferred_element_type=F32)


def _layernorm(x, g, b):
    mu = jnp.mean(x, axis=-1, keepdims=True)
    xc = x - mu
    var = jnp.mean(xc * xc, axis=-1, keepdims=True)
    return xc * lax.rsqrt(var + LN_EPS) * g + b


def _resident(shape):
    nd = len(shape)
    return pl.BlockSpec(shape, lambda *_: (0,) * nd, pipeline_mode=pl.Buffered(1))


def _params(*sem):
    return pltpu.CompilerParams(dimension_semantics=sem, vmem_limit_bytes=VMEM_LIMIT)


ZB_Q, ZB_H1F, ZB_H2F, ZB_KF, ZB_H1B, ZB_H2B, ZB_KB, ZB_V, ZB_G, ZB_QATTN = range(10)
ZB_WIDTH = 10 * A_WIDTH
assert B_WIDTH == A_WIDTH
ZF_WIDTH = 2 * B_KV_WIDTH


def _forget_gate(x, log_lb, l1m):
    log_sig = jnp.minimum(x, 0.0) - jnp.log(1.0 + jnp.exp(-jnp.abs(x)))
    c = l1m + log_sig
    logf = jnp.maximum(log_lb, c) + jnp.log(1.0 + jnp.exp(-jnp.abs(log_lb - c)))
    lf2 = logf * LOG2E
    hi = lf2.astype(BF16)
    lo = (lf2 - hi.astype(F32)).astype(BF16)
    k = jnp.exp(c - x)
    return hi, lo, k.astype(BF16)


def _inproj_body(x_ref, w_ref, gc_ref, zb_ref, zf_ref):
    step = MXU_DIM
    per_group = A_WIDTH // step
    n_chunks = AB_IN // step
    gate_chunks = list(range(per_group, 3 * per_group))
    other_chunks = [c for c in range(n_chunks) if c not in gate_chunks]
    order = []
    while gate_chunks or other_chunks:
        order += gate_chunks[:1] + other_chunks[:2]
        gate_chunks, other_chunks = gate_chunks[1:], other_chunks[2:]
    xb = x_ref[...].astype(BF16)
    for c in order:
        z = _dot(xb, w_ref[:, c * step:(c + 1) * step])
        group, part = divmod(c, per_group)
        sub = slice(part * step, (part + 1) * step)

        def put(zb_group, val):
            zb_ref[:, zb_group * A_WIDTH + part * step:zb_group * A_WIDTH + (part + 1) * step] = val

        if group == 0:
            put(ZB_Q, (z * jax.nn.sigmoid(z)).astype(BF16))
        elif group in (1, 2):
            d = group - 1
            hi, lo, k = _forget_gate(z, gc_ref[d, 0:1, sub], gc_ref[d, 1:2, sub])
            put(ZB_H1F if d == 0 else ZB_H1B, hi)
            put(ZB_H2F if d == 0 else ZB_H2B, lo)
            put(ZB_KF if d == 0 else ZB_KB, k)
        elif group in (3, 4, 5):
            put({3: ZB_V, 4: ZB_G, 5: ZB_QATTN}[group], z.astype(BF16))
        else:
            zf_ref[...] = z


def _inproj(x, w, gate_consts):
    n = x.shape[0]
    tm = TOKEN_TILE
    return pl.pallas_call(
        _inproj_body,
        grid=(n // tm,),
        in_specs=[pl.BlockSpec((tm, D_MODEL), lambda i: (i, 0)), _resident((D_MODEL, AB_IN)),
                  _resident((2, 2, A_WIDTH))],
        out_specs=[pl.BlockSpec((tm, ZB_WIDTH), lambda i: (i, 0)), pl.BlockSpec((tm, ZF_WIDTH), lambda i: (i, 0))],
        out_shape=[jax.ShapeDtypeStruct((n, ZB_WIDTH), BF16), jax.ShapeDtypeStruct((n, ZF_WIDTH), F32)],
        compiler_params=_params("parallel"),
        name="inproj",
    )(x, w, gate_consts)


N_LEVELS = SCAN_CHUNK.bit_length() - 1
SUBLANES = 8
SCAN_HEADS = 2
MXU_LEVELS = 2


def _level_matrix(lvl):
    c = SCAN_CHUNK
    h = c >> lvl
    t = np.arange(c)[:, None]
    u = np.arange(c)[None, :]
    boundary = (t // (2 * h)) * 2 * h + h - 1
    second = (t % (2 * h)) >= h
    return np.where(second, (u > boundary) & (u <= t), (u > t) & (u <= boundary))


def _scan_tables():
    c = SCAN_CHUNK
    t = np.arange(c)[:, None]
    u = np.arange(c)[None, :]
    level = np.where(u == t, 0, -1)
    for lvl in range(1, N_LEVELS + 1):
        h = c >> lvl
        level = np.where((u < t) & ((t ^ u) >= h) & ((t ^ u) < 2 * h), lvl, level)
    mats = [u <= t] + [_level_matrix(lvl) for lvl in range(N_LEVELS - MXU_LEVELS + 1, N_LEVELS + 1)]
    fwd = np.concatenate(mats, axis=0)
    bwd = np.concatenate([m[::-1, ::-1] for m in mats], axis=0)
    rs = np.stack([fwd, bwd]).astype(np.float32)
    rs2 = jnp.asarray(np.concatenate([rs, rs], axis=2), dtype=BF16)
    lv = jnp.asarray(np.stack([level, level.T]).astype(np.int32))
    return rs2, lv


def _neg_abs(x):
    bits = lax.bitcast_convert_type(x, jnp.uint32) | jnp.uint32(0x80000000)
    return lax.bitcast_convert_type(bits, F32)


def _level_exponents(xb, xb_ref, lvl, rev):
    cs = SCAN_CHUNK
    h = cs >> lvl
    assert 2 * h >= SUBLANES
    pieces = []
    for start in range(0, cs, 2 * h):
        bnd = start + (h if rev else h - 1)
        pieces.append(_neg_abs(xb[start:start + 2 * h] - xb_ref[bnd:bnd + 1, :]))
    return jnp.concatenate(pieces, axis=0)


def _hgrn_chunks(q, k, v, hilo, sts, rs2, level, xb_ref):
    cs = SCAN_CHUNK
    dirs = (0, 1)
    heads = [slice(hd * A_HEAD_DIM, (hd + 1) * A_HEAD_DIM) for hd in range(SCAN_HEADS)]
    x = [_dot(rs2[d], hilo[d]) for d in dirs]
    xb = [x[d][0:cs] for d in dirs]
    for d in dirs:
        xb_ref[d] = xb[d]
    x_last = [xb_ref[d, (0 if d else cs - 1):(1 if d else cs), :] for d in dirs]

    k_t = [[k[d][:, hs].T for hs in heads] for d in dirs]
    a = [[jnp.where(level[d] == 0, _dot(q[d][:, hs], kt), 0.0) for hs, kt in zip(heads, k_t[d])] for d in dirs]
    for lvl in range(1, N_LEVELS + 1):
        from_mxu = lvl - (N_LEVELS - MXU_LEVELS)
        for d in dirs:
            if from_mxu >= 1:
                e = x[d][from_mxu * cs:(from_mxu + 1) * cs]
            else:
                e = _level_exponents(xb[d], xb_ref.at[d], lvl, bool(d))
            p = jnp.exp2(e).astype(BF16)
            qp = q[d] * p
            a[d] = [jnp.where(level[d] == lvl, _dot(qp[:, hs], kt * p[:, hs].T), a_h)
                    for hs, kt, a_h in zip(heads, k_t[d], a[d])]

    outs, new_sts = [], []
    for d in dirs:
        q_st = q[d] * jnp.exp2(xb[d]).astype(BF16)
        k_st = k[d] * jnp.exp2(_neg_abs(xb[d] - x_last[d])).astype(BF16)
        decay = jnp.exp2(x_last[d])
        outs.append([_dot(a_h.astype(BF16), v[d][:, hs]) + _dot_nt(q_st[:, hs], st.astype(BF16))
                     for hs, a_h, st in zip(heads, a[d], sts[d])])
        new_sts.append([st * decay[:, hs] + _dot_tn(v[d][:, hs], k_st[:, hs]) for hs, st in zip(heads, sts[d])])
    return outs, new_sts


def _hgrn_body(qf_ref, h1f_ref, h2f_ref, kf_ref, vf_ref, qb_ref, h1b_ref, h2b_ref, kb_ref, vb_ref,
               rs_ref, lv_ref, of_ref, ob_ref, st_ref, xb_ref):
    @pl.when(pl.program_id(2) == 0)
    def _():
        st_ref[...] = jnp.zeros_like(st_ref)

    cs = SCAN_CHUNK
    n_chunks = SCAN_TILE // cs
    dirs = (0, 1)
    q_refs, h1_refs, h2_refs = (qf_ref, qb_ref), (h1f_ref, h1b_ref), (h2f_ref, h2b_ref)
    k_refs, v_refs, o_refs = (kf_ref, kb_ref), (vf_ref, vb_ref), (of_ref, ob_ref)

    def step(j, carry):
        rows = [pl.ds(pl.multiple_of((n_chunks - 1 - j if d else j) * cs, cs), cs) for d in dirs]
        hilo = [jnp.concatenate([h1_refs[d][rows[d], :], h2_refs[d][rows[d], :]], axis=0) for d in dirs]
        outs, sts = _hgrn_chunks([q_refs[d][rows[d], :] for d in dirs], [k_refs[d][rows[d], :] for d in dirs],
                                 [v_refs[d][rows[d], :] for d in dirs], hilo,
                                 [[st_ref[d, hd] for hd in range(SCAN_HEADS)] for d in dirs],
                                 [rs_ref[d] for d in dirs], [lv_ref[d] for d in dirs], xb_ref)
        for d in dirs:
            for hd in range(SCAN_HEADS):
                o_refs[d][rows[d], hd * A_HEAD_DIM:(hd + 1) * A_HEAD_DIM] = outs[d][hd].astype(BF16)
                st_ref[d, hd] = sts[d][hd]
        return carry

    lax.fori_loop(0, n_chunks, step, 0, unroll=2)


def _hgrn(zb, batch, seq):
    nt = seq // SCAN_TILE
    width = SCAN_HEADS * A_HEAD_DIM
    hw = A_HEADS // SCAN_HEADS
    rs_tab, lv_tab = _scan_tables()

    def fwd(group):
        return pl.BlockSpec((SCAN_TILE, width), lambda b, h, i: (b * nt + i, group * hw + h))

    def bwd(group):
        return pl.BlockSpec((SCAN_TILE, width), lambda b, h, i: (b * nt + nt - 1 - i, group * hw + h))

    out = jax.ShapeDtypeStruct((batch * seq, A_WIDTH), BF16)
    state = (2, SCAN_HEADS, A_HEAD_DIM, A_HEAD_DIM)
    return pl.pallas_call(
        _hgrn_body,
        grid=(batch, hw, nt),
        in_specs=[fwd(ZB_Q), fwd(ZB_H1F), fwd(ZB_H2F), fwd(ZB_KF), fwd(ZB_V),
                  bwd(ZB_Q), bwd(ZB_H1B), bwd(ZB_H2B), bwd(ZB_KB), bwd(ZB_V),
                  _resident(rs_tab.shape), _resident(lv_tab.shape)],
        out_specs=[pl.BlockSpec((SCAN_TILE, width), lambda b, h, i: (b * nt + i, h)),
                   pl.BlockSpec((SCAN_TILE, width), lambda b, h, i: (b * nt + nt - 1 - i, h))],
        out_shape=[out, out],
        scratch_shapes=[pltpu.VMEM(state, F32), pltpu.VMEM((2, SCAN_CHUNK, width), F32)],
        compiler_params=_params("parallel", "parallel", "arbitrary"),
        name="hgrn2_scan",
    )(*([zb] * 10), rs_tab, lv_tab)


HALF = LANE // 2


def _attn_bias(kvh, second):
    blk = ATTN_BLOCK
    t = lax.broadcasted_iota(jnp.int32, (2 * blk, 3 * blk), 0)
    s = lax.broadcasted_iota(jnp.int32, (2 * blk, 3 * blk), 1)
    upper = t >= blk
    dist = jnp.abs(jnp.where(upper, t - blk, t) - s + blk)
    head_lo = kvh * B_GROUP + second
    slope_lo = LOG2E * 2.0 ** (-(8.0 / B_HEADS) * (head_lo + 1))
    slope_hi = LOG2E * 2.0 ** (-(8.0 / B_HEADS) * (head_lo + 3))
    slope = jnp.where(upper, slope_hi, slope_lo)
    return jnp.where(dist <= WINDOW, -slope * dist.astype(F32), -jnp.inf)


def _attn_body(sink_ref, q_ref, kc_ref, kp_ref, kn_ref, vc_ref, vp_ref, vn_ref, o_ref, kv_scr, bias_scr):
    i = pl.program_id(1)
    blk = ATTN_BLOCK

    @pl.when((pl.program_id(0) == 0) & (i == 0))
    def _():
        for kvh in range(B_KV_HEADS):
            for second in range(2):
                bias_scr[kvh, second] = _attn_bias(kvh, second)

    lane = lax.broadcasted_iota(jnp.int32, (ATTN_TILE + 2 * blk, LANE), 1)
    low = lane < HALF
    q_scale = LOG2E * B_HEAD_DIM ** -0.5
    for x, (p_ref, c_ref, n_ref) in enumerate(((kp_ref, kc_ref, kn_ref), (vp_ref, vc_ref, vn_ref))):
        full = jnp.concatenate([p_ref[...], c_ref[...], n_ref[...]], axis=0)
        if x == 0:
            full = full * q_scale
        swapped = pltpu.roll(full, HALF, 1)
        kv_scr[x, 0, 0] = jnp.where(low, full, 0.0).astype(BF16)
        kv_scr[x, 0, 1] = jnp.where(low, 0.0, swapped).astype(BF16)
        kv_scr[x, 1, 0] = jnp.where(low, swapped, 0.0).astype(BF16)
        kv_scr[x, 1, 1] = jnp.where(low, 0.0, full).astype(BF16)

    s_idx = lax.broadcasted_iota(jnp.int32, (2 * blk, 3 * blk), 1)
    key_lo = jnp.where(i == 0, blk, 0)
    key_hi = jnp.where(i == pl.num_programs(1) - 1, 2 * blk, 3 * blk)
    upper = lax.broadcasted_iota(jnp.int32, (2 * blk, 1), 0) >= blk
    out_low = lax.broadcasted_iota(jnp.int32, (2 * blk, LANE), 1) < HALF
    n_blocks = ATTN_TILE // blk
    for j in range(n_blocks):
        rows = slice(j * blk, (j + 1) * blk)
        keys = slice(j * blk, (j + 3) * blk)
        for kvh in range(B_KV_HEADS):
            c0 = kvh * B_GROUP * B_HEAD_DIM
            q2 = jnp.concatenate([q_ref[rows, c0:c0 + LANE], q_ref[rows, c0 + LANE:c0 + 2 * LANE]], axis=0)
            probs, denoms = [], []
            for second in range(2):
                sc = _dot_nt(q2, kv_scr[0, kvh, second, keys, :]) + bias_scr[kvh, second]
                if j == 0:
                    sc = jnp.where(s_idx >= key_lo, sc, -jnp.inf)
                if j == n_blocks - 1:
                    sc = jnp.where(s_idx < key_hi, sc, -jnp.inf)
                head = kvh * B_GROUP + second
                sink = LOG2E * jnp.where(upper, sink_ref[head + 2], sink_ref[head])
                m = jnp.maximum(jnp.max(sc, axis=-1, keepdims=True), sink)
                p = jnp.exp2(sc - m)
                denoms.append(jnp.sum(p, axis=-1, keepdims=True) + jnp.exp2(sink - m))
                probs.append(p.astype(BF16))
            o = _dot(probs[0], kv_scr[1, kvh, 0, keys, :]) + _dot(probs[1], kv_scr[1, kvh, 1, keys, :])
            o = (o / jnp.where(out_low, denoms[0], denoms[1])).astype(BF16)
            o_ref[rows, c0:c0 + LANE] = o[0:blk]
            o_ref[rows, c0 + LANE:c0 + 2 * LANE] = o[blk:]


def _attn(zb, zf, sink, batch, seq):
    nt = seq // ATTN_TILE
    per = ATTN_TILE // ATTN_BLOCK
    nblk = seq // ATTN_BLOCK
    k_col, v_col = 0, 1

    def cur(col):
        return pl.BlockSpec((ATTN_TILE, B_KV_WIDTH), lambda b, i: (b * nt + i, col))

    def prev(col):
        return pl.BlockSpec((ATTN_BLOCK, B_KV_WIDTH),
                            lambda b, i: (b * nblk + jnp.maximum(i * per - 1, 0), col))

    def nxt(col):
        return pl.BlockSpec((ATTN_BLOCK, B_KV_WIDTH),
                            lambda b, i: (b * nblk + jnp.minimum((i + 1) * per, nblk - 1), col))

    return pl.pallas_call(
        _attn_body,
        grid=(batch, nt),
        in_specs=[pl.BlockSpec(memory_space=pltpu.SMEM),
                  pl.BlockSpec((ATTN_TILE, B_WIDTH), lambda b, i: (b * nt + i, ZB_QATTN)),
                  cur(k_col), prev(k_col), nxt(k_col), cur(v_col), prev(v_col), nxt(v_col)],
        out_specs=pl.BlockSpec((ATTN_TILE, B_WIDTH), lambda b, i: (b * nt + i, 0)),
        out_shape=jax.ShapeDtypeStruct((batch * seq, B_WIDTH), BF16),
        scratch_shapes=[pltpu.VMEM((2, B_KV_HEADS, 2, ATTN_TILE + 2 * ATTN_BLOCK, LANE), BF16),
                        pltpu.VMEM((B_KV_HEADS, 2, 2 * ATTN_BLOCK, 3 * ATTN_BLOCK), F32)],
        compiler_params=_params("arbitrary", "arbitrary"),
        name="window_attn",
    )(sink, zb, zf, zf, zf, zf, zf, zf)


def _outproj_body(x_ref, of_ref, ob_ref, g_ref, oattn_ref, ng_ref, w_ref, lg_ref, lb_ref, y_ref):
    n_blocks = FFN_TILE // ROW_BLOCK

    def rows_of(r):
        return slice(r * ROW_BLOCK, (r + 1) * ROW_BLOCK)

    def mixer_out(r):
        rows = rows_of(r)
        o = of_ref[rows, :].astype(F32) + ob_ref[rows, :].astype(F32)
        parts = []
        for h in range(A_HEADS):
            sl = slice(h * A_HEAD_DIM, (h + 1) * A_HEAD_DIM)
            oh = o[:, sl]
            oh = oh * lax.rsqrt(jnp.mean(oh * oh, axis=-1, keepdims=True) + RMS_EPS) * ng_ref[...]
            gh = g_ref[rows, sl].astype(F32)
            parts.append((oh * (gh * jax.nn.sigmoid(gh))).astype(BF16))
        return jnp.concatenate(parts + [oattn_ref[rows, :]], axis=-1)

    mixed = mixer_out(0)
    for r in range(n_blocks):
        y = _dot(mixed, w_ref[...])
        if r + 1 < n_blocks:
            mixed = mixer_out(r + 1)
        y_ref[rows_of(r), :] = _layernorm(DEEPNORM_ALPHA * x_ref[rows_of(r), :] + y, lg_ref[...], lb_ref[...])


def _outproj(x, o_f, o_b, z, o_attn, norm_g, w, ln_g, ln_b):
    n = x.shape[0]
    tm = FFN_TILE
    g_col = ZB_G
    return pl.pallas_call(
        _outproj_body,
        grid=(n // tm,),
        in_specs=[pl.BlockSpec((tm, D_MODEL), lambda i: (i, 0)),
                  pl.BlockSpec((tm, A_WIDTH), lambda i: (i, 0)),
                  pl.BlockSpec((tm, A_WIDTH), lambda i: (i, 0)),
                  pl.BlockSpec((tm, A_WIDTH), lambda i: (i, g_col)),
                  pl.BlockSpec((tm, B_WIDTH), lambda i: (i, 0)),
                  _resident((1, A_HEAD_DIM)), _resident((A_WIDTH + B_WIDTH, D_MODEL)),
                  _resident((1, D_MODEL)), _resident((1, D_MODEL))],
        out_specs=pl.BlockSpec((tm, D_MODEL), lambda i: (i, 0)),
        out_shape=jax.ShapeDtypeStruct((n, D_MODEL), F32),
        compiler_params=_params("parallel"),
        name="outproj_ln",
    )(x, o_f, o_b, z, o_attn, norm_g, w, ln_g, ln_b)


def _sgu_body(x_ref, win_ref, cg_ref, cb_ref, ws_ref, bs_ref, wout_ref, lg_ref, lb_ref, y_ref,
              u_scr, v_scr, h_scr):
    step = 2 * MXU_DIM
    n_blocks = FFN_TILE // ROW_BLOCK

    def rows_of(r):
        return slice(r * ROW_BLOCK, (r + 1) * ROW_BLOCK)

    def project(r):
        u_blk, v_blk = u_scr.at[r % 2], v_scr.at[r % 2]
        xb = x_ref[rows_of(r), :].astype(BF16)
        for c in range(2 * C_WIDTH // step):
            z = _dot(xb, win_ref[:, c * step:(c + 1) * step])
            z = 0.5 * z * (1.0 + lax.erf(z * (2.0 ** -0.5)))
            if c < C_WIDTH // step:
                u_blk[:, c * step:(c + 1) * step] = z
            else:
                v_blk[:, c * step - C_WIDTH:(c + 1) * step - C_WIDTH] = z

    def gate(r):
        u_blk, v_blk, h_blk = u_scr.at[r % 2], v_scr.at[r % 2], h_scr.at[r % 2]
        v_blk[...] = _layernorm(v_blk[...], cg_ref[...], cb_ref[...])
        for g in range(C_GROUPS):
            w_g = ws_ref[g]
            bias = bs_ref[:, g:g + 1]
            cols = slice(g * C_GROUP_DIM, (g + 1) * C_GROUP_DIM)
            for n in range(ROW_BLOCK // C_CHUNK):
                chunk = slice(n * C_CHUNK, (n + 1) * C_CHUNK)
                s = _dot(w_g, v_blk[chunk, cols].astype(BF16)) + bias
                h_blk[chunk, cols] = (u_blk[chunk, cols] * s).astype(BF16)
        return _dot(h_blk[...], wout_ref[...])

    project(0)
    project(1)
    for r in range(n_blocks):
        y = gate(r)
        if r + 2 < n_blocks:
            project(r + 2)
        y_ref[rows_of(r), :] = _layernorm(DEEPNORM_ALPHA * x_ref[rows_of(r), :] + y, lg_ref[...], lb_ref[...])


def _sgu(x, w_in, c_g, c_b, w_s, b_s_t, w_out, ln_g, ln_b):
    n = x.shape[0]
    tm = FFN_TILE
    return pl.pallas_call(
        _sgu_body,
        grid=(n // tm,),
        in_specs=[pl.BlockSpec((tm, D_MODEL), lambda i: (i, 0)),
                  _resident((D_MODEL, 2 * C_WIDTH)), _resident((1, C_WIDTH)), _resident((1, C_WIDTH)),
                  _resident((C_GROUPS, C_CHUNK, C_CHUNK)), _resident((C_CHUNK, C_GROUPS)),
                  _resident((C_WIDTH, D_MODEL)), _resident((1, D_MODEL)), _resident((1, D_MODEL))],
        out_specs=pl.BlockSpec((tm, D_MODEL), lambda i: (i, 0)),
        out_shape=jax.ShapeDtypeStruct((n, D_MODEL), F32),
        scratch_shapes=[pltpu.VMEM((2, ROW_BLOCK, C_WIDTH), F32), pltpu.VMEM((2, ROW_BLOCK, C_WIDTH), F32),
                        pltpu.VMEM((2, ROW_BLOCK, C_WIDTH), BF16)],
        compiler_params=_params("parallel"),
        name="sgu_ln",
    )(x, w_in, c_g, c_b, w_s, b_s_t, w_out, ln_g, ln_b)


def _ffn_body(x_ref, wg_ref, wu_ref, wd_ref, lg_ref, lb_ref, y_ref, h_scr):
    for r in range(FFN_TILE // ROW_BLOCK):
        rows = slice(r * ROW_BLOCK, (r + 1) * ROW_BLOCK)
        h_blk = h_scr.at[r % 2]
        x = x_ref[rows, :]
        xb = x.astype(BF16)
        for c in range(FFN_DIM // MXU_DIM):
            sl = slice(c * MXU_DIM, (c + 1) * MXU_DIM)
            gate = _dot(xb, wg_ref[:, sl])
            up = _dot(xb, wu_ref[:, sl])
            h_blk[:, sl] = (gate * jax.nn.sigmoid(gate) * up).astype(BF16)
        y = _dot(h_blk[...], wd_ref[...])
        y_ref[rows, :] = _layernorm(DEEPNORM_ALPHA * x + y, lg_ref[...], lb_ref[...])


def _ffn(x, wg, wu, wd, ln_g, ln_b):
    n = x.shape[0]
    tm = FFN_TILE
    return pl.pallas_call(
        _ffn_body,
        grid=(n // tm,),
        in_specs=[pl.BlockSpec((tm, D_MODEL), lambda i: (i, 0)),
                  _resident((D_MODEL, FFN_DIM)), _resident((D_MODEL, FFN_DIM)), _resident((FFN_DIM, D_MODEL)),
                  _resident((1, D_MODEL)), _resident((1, D_MODEL))],
        out_specs=pl.BlockSpec((tm, D_MODEL), lambda i: (i, 0)),
        out_shape=jax.ShapeDtypeStruct((n, D_MODEL), F32),
        scratch_shapes=[pltpu.VMEM((2, ROW_BLOCK, FFN_DIM), BF16)],
        compiler_params=_params("parallel"),
        name="swiglu_ln",
    )(x, wg, wu, wd, ln_g, ln_b)


def _trunk(x, p):
    batch, seq, _ = x.shape
    x = x.reshape(batch * seq, D_MODEL)
    for layer in range(DEPTH):
        j = layer // 2
        if layer % 2 == 0:
            zb, zf = _inproj(x, p["w_in_ab"][j], p["gate_consts"][j])
            o_f, o_b = _hgrn(zb, batch, seq)
            o_attn = _attn(zb, zf, p["attn_sink"][j], batch, seq)
            x = _outproj(x, o_f, o_b, zb, o_attn, p["hgrn_norm_g"][j], p["w_out_ab"][j],
                         p["ln_mix_g"][layer], p["ln_mix_b"][layer])
        else:
            x = _sgu(x, p["w_in_c"][j], p["c_ln_g"][j], p["c_ln_b"][j], p["c_ws"][j], p["c_bs_t"][j],
                     p["w_out_c"][j], p["ln_mix_g"][layer], p["ln_mix_b"][layer])
        x = _ffn(x, p["ffn_w_gate"][layer], p["ffn_w_up"][layer], p["ffn_w_down"][layer],
                 p["ln_ffn_g"][layer], p["ln_ffn_b"][layer])
    return x.reshape(batch, seq, D_MODEL)


def kernel(x_prompt, x_sample, w_in_ab, hgrn_lb_logits, hgrn_norm_g, attn_sink, w_out_ab, w_in_c, c_ln_g,
           c_ln_b, c_ws, c_bs, w_out_c, ffn_w_gate, ffn_w_up, ffn_w_down, ln_mix_g, ln_mix_b, ln_ffn_g,
           ln_ffn_b):
    prob = jax.nn.softmax(hgrn_lb_logits.astype(F32), axis=0)
    lower = jnp.maximum(jnp.cumsum(prob, axis=0) - prob[0:1], 0.0)
    gate_consts = jnp.stack([jnp.log(lower), jnp.log1p(-lower)], axis=2)
    row = lambda a: a.astype(F32)[:, None, :]
    p = dict(
        w_in_ab=w_in_ab.astype(BF16), gate_consts=gate_consts, hgrn_norm_g=row(hgrn_norm_g),
        attn_sink=attn_sink.astype(F32), w_out_ab=w_out_ab.astype(BF16),
        w_in_c=w_in_c.astype(BF16), c_ln_g=row(c_ln_g), c_ln_b=row(c_ln_b), c_ws=c_ws.astype(BF16),
        c_bs_t=jnp.swapaxes(c_bs.astype(F32), 1, 2), w_out_c=w_out_c.astype(BF16),
        ffn_w_gate=ffn_w_gate.astype(BF16), ffn_w_up=ffn_w_up.astype(BF16), ffn_w_down=ffn_w_down.astype(BF16),
        ln_mix_g=row(ln_mix_g), ln_mix_b=row(ln_mix_b), ln_ffn_g=row(ln_ffn_g), ln_ffn_b=row(ln_ffn_b),
    )
    return (_trunk(x_prompt, p), _trunk(x_sample, p))
```

```python
import math

import numpy as np
import jax
import jax.numpy as jnp
from jax import lax
from jax.experimental import pallas as pl
from jax.experimental.pallas import tpu as pltpu

D_MODEL = 1024
DEPTH = 4
A_HEADS = 4
A_HEAD_DIM = 128
A_WIDTH = A_HEADS * A_HEAD_DIM
B_HEADS = 8
B_KV_HEADS = 2
B_GROUP = B_HEADS // B_KV_HEADS
B_HEAD_DIM = 64
B_WIDTH = B_HEADS * B_HEAD_DIM
B_KV_WIDTH = B_KV_HEADS * B_HEAD_DIM
WINDOW = 128
ATTN_BLOCK = 128
AB_IN = 5 * A_WIDTH + B_WIDTH + 2 * B_KV_WIDTH
C_WIDTH = 2 * D_MODEL
C_GROUPS = 8
C_GROUP_DIM = C_WIDTH // C_GROUPS
C_CHUNK = 128
FFN_DIM = 2816
DEEPNORM_ALPHA = (2.0 * DEPTH) ** 0.25
LN_EPS = 1e-5
RMS_EPS = 1e-6

LANE = 128
MXU_DIM = 256
VMEM_LIMIT = 56 * 1024 * 1024

TOKEN_TILE = 512
FFN_TILE = 1024
ROW_BLOCK = 256
SCAN_TILE = 1024
SCAN_CHUNK = 128
ATTN_TILE = 512

F32 = jnp.float32
BF16 = jnp.bfloat16
LOG2E = math.log2(math.e)


def _dot(a, b):
    return jnp.dot(a, b, preferred_element_type=F32)


def _dot_nt(a, b):
    return lax.dot_general(a, b, (((1,), (1,)), ((), ())), preferred_element_type=F32)


def _dot_tn(a, b):
    return lax.dot_general(a, b, (((0,), (0,)), ((), ())), preferred_element_type=F32)


def _layernorm(x, g, b):
    mu = jnp.mean(x, axis=-1, keepdims=True)
    xc = x - mu
    var = jnp.mean(xc * xc, axis=-1, keepdims=True)
    return xc * lax.rsqrt(var + LN_EPS) * g + b


def _resident(shape):
    nd = len(shape)
    return pl.BlockSpec(shape, lambda *_: (0,) * nd, pipeline_mode=pl.Buffered(1))


def _params(*sem):
    return pltpu.CompilerParams(dimension_semantics=sem, vmem_limit_bytes=VMEM_LIMIT)


ZB_Q, ZB_H1F, ZB_H2F, ZB_KF, ZB_H1B, ZB_H2B, ZB_KB, ZB_V, ZB_G, ZB_QATTN = range(10)
ZB_WIDTH = 10 * A_WIDTH
assert B_WIDTH == A_WIDTH
ZF_WIDTH = 2 * B_KV_WIDTH


def _forget_gate(x, log_lb, l1m):
    log_sig = jnp.minimum(x, 0.0) - jnp.log(1.0 + jnp.exp(-jnp.abs(x)))
    c = l1m + log_sig
    logf = jnp.maximum(log_lb, c) + jnp.log(1.0 + jnp.exp(-jnp.abs(log_lb - c)))
    lf2 = logf * LOG2E
    hi = lf2.astype(BF16)
    lo = (lf2 - hi.astype(F32)).astype(BF16)
    k = jnp.exp(c - x)
    return hi, lo, k.astype(BF16)


def _inproj_body(x_ref, w_ref, gc_ref, zb_ref, zf_ref):
    step = MXU_DIM
    per_group = A_WIDTH // step
    n_chunks = AB_IN // step
    gate_chunks = list(range(per_group, 3 * per_group))
    other_chunks = [c for c in range(n_chunks) if c not in gate_chunks]
    order = []
    while gate_chunks or other_chunks:
        order += gate_chunks[:1] + other_chunks[:2]
        gate_chunks, other_chunks = gate_chunks[1:], other_chunks[2:]
    xb = x_ref[...].astype(BF16)
    for c in order:
        z = _dot(xb, w_ref[:, c * step:(c + 1) * step])
        group, part = divmod(c, per_group)
        sub = slice(part * step, (part + 1) * step)

        def put(zb_group, val):
            zb_ref[:, zb_group * A_WIDTH + part * step:zb_group * A_WIDTH + (part + 1) * step] = val

        if group == 0:
            put(ZB_Q, (z * jax.nn.sigmoid(z)).astype(BF16))
        elif group in (1, 2):
            d = group - 1
            hi, lo, k = _forget_gate(z, gc_ref[d, 0:1, sub], gc_ref[d, 1:2, sub])
            put(ZB_H1F if d == 0 else ZB_H1B, hi)
            put(ZB_H2F if d == 0 else ZB_H2B, lo)
            put(ZB_KF if d == 0 else ZB_KB, k)
        elif group in (3, 4, 5):
            put({3: ZB_V, 4: ZB_G, 5: ZB_QATTN}[group], z.astype(BF16))
        else:
            zf_ref[...] = z


def _inproj(x, w, gate_consts):
    n = x.shape[0]
    tm = TOKEN_TILE
    return pl.pallas_call(
        _inproj_body,
        grid=(n // tm,),
        in_specs=[pl.BlockSpec((tm, D_MODEL), lambda i: (i, 0)), _resident((D_MODEL, AB_IN)),
                  _resident((2, 2, A_WIDTH))],
        out_specs=[pl.BlockSpec((tm, ZB_WIDTH), lambda i: (i, 0)), pl.BlockSpec((tm, ZF_WIDTH), lambda i: (i, 0))],
        out_shape=[jax.ShapeDtypeStruct((n, ZB_WIDTH), BF16), jax.ShapeDtypeStruct((n, ZF_WIDTH), F32)],
        compiler_params=_params("parallel"),
        name="inproj",
    )(x, w, gate_consts)


N_LEVELS = SCAN_CHUNK.bit_length() - 1
SUBLANES = 8
SCAN_HEADS = 4
MXU_LEVELS = 2


def _level_matrix(lvl):
    c = SCAN_CHUNK
    h = c >> lvl
    t = np.arange(c)[:, None]
    u = np.arange(c)[None, :]
    boundary = (t // (2 * h)) * 2 * h + h - 1
    second = (t % (2 * h)) >= h
    return np.where(second, (u > boundary) & (u <= t), (u > t) & (u <= boundary))


def _scan_tables():
    c = SCAN_CHUNK
    t = np.arange(c)[:, None]
    u = np.arange(c)[None, :]
    level = np.where(u == t, 0, -1)
    for lvl in range(1, N_LEVELS + 1):
        h = c >> lvl
        level = np.where((u < t) & ((t ^ u) >= h) & ((t ^ u) < 2 * h), lvl, level)
    mats = [u <= t] + [_level_matrix(lvl) for lvl in range(N_LEVELS - MXU_LEVELS + 1, N_LEVELS + 1)]
    fwd = np.concatenate(mats, axis=0)
    bwd = np.concatenate([m[::-1, ::-1] for m in mats], axis=0)
    rs = np.stack([fwd, bwd]).astype(np.float32)
    rs2 = jnp.asarray(np.concatenate([rs, rs], axis=2), dtype=BF16)
    lv = jnp.asarray(np.stack([level, level.T]).astype(np.int32))
    return rs2, lv


def _neg_abs(x):
    bits = lax.bitcast_convert_type(x, jnp.uint32) | jnp.uint32(0x80000000)
    return lax.bitcast_convert_type(bits, F32)


def _level_exponents(xb, xb_ref, lvl, rev):
    cs = SCAN_CHUNK
    h = cs >> lvl
    assert 2 * h >= SUBLANES
    pieces = []
    for start in range(0, cs, 2 * h):
        bnd = start + (h if rev else h - 1)
        pieces.append(_neg_abs(xb[start:start + 2 * h] - xb_ref[bnd:bnd + 1, :]))
    return jnp.concatenate(pieces, axis=0)


def _hgrn_chunks(q, k, v, hilo, sts, rs2, level, xb_ref):
    cs = SCAN_CHUNK
    dirs = (0, 1)
    heads = [slice(hd * A_HEAD_DIM, (hd + 1) * A_HEAD_DIM) for hd in range(SCAN_HEADS)]
    x = [_dot(rs2[d], hilo[d]) for d in dirs]
    xb = [x[d][0:cs] for d in dirs]
    for d in dirs:
        xb_ref[d] = xb[d]
    x_last = [xb_ref[d, (0 if d else cs - 1):(1 if d else cs), :] for d in dirs]

    k_t = [[k[d][:, hs].T for hs in heads] for d in dirs]
    a = [[jnp.where(level[d] == 0, _dot(q[d][:, hs], kt), 0.0) for hs, kt in zip(heads, k_t[d])] for d in dirs]
    for lvl in range(1, N_LEVELS + 1):
        from_mxu = lvl - (N_LEVELS - MXU_LEVELS)
        for d in dirs:
            if from_mxu >= 1:
                e = x[d][from_mxu * cs:(from_mxu + 1) * cs]
            else:
                e = _level_exponents(xb[d], xb_ref.at[d], lvl, bool(d))
            p = jnp.exp2(e).astype(BF16)
            qp = q[d] * p
            a[d] = [jnp.where(level[d] == lvl, _dot(qp[:, hs], kt * p[:, hs].T), a_h)
                    for hs, kt, a_h in zip(heads, k_t[d], a[d])]

    outs, new_sts = [], []
    for d in dirs:
        q_st = q[d] * jnp.exp2(xb[d]).astype(BF16)
        k_st = k[d] * jnp.exp2(_neg_abs(xb[d] - x_last[d])).astype(BF16)
        decay = jnp.exp2(x_last[d])
        outs.append([_dot(a_h.astype(BF16), v[d][:, hs]) + _dot_nt(q_st[:, hs], st.astype(BF16))
                     for hs, a_h, st in zip(heads, a[d], sts[d])])
        new_sts.append([st * decay[:, hs] + _dot_tn(v[d][:, hs], k_st[:, hs]) for hs, st in zip(heads, sts[d])])
    return outs, new_sts


def _hgrn_body(qf_ref, h1f_ref, h2f_ref, kf_ref, vf_ref, qb_ref, h1b_ref, h2b_ref, kb_ref, vb_ref,
               rs_ref, lv_ref, of_ref, ob_ref, st_ref, xb_ref):
    @pl.when(pl.program_id(2) == 0)
    def _():
        st_ref[...] = jnp.zeros_like(st_ref)

    cs = SCAN_CHUNK
    n_chunks = SCAN_TILE // cs
    dirs = (0, 1)
    q_refs, h1_refs, h2_refs = (qf_ref, qb_ref), (h1f_ref, h1b_ref), (h2f_ref, h2b_ref)
    k_refs, v_refs, o_refs = (kf_ref, kb_ref), (vf_ref, vb_ref), (of_ref, ob_ref)

    def step(j, carry):
        rows = [pl.ds(pl.multiple_of((n_chunks - 1 - j if d else j) * cs, cs), cs) for d in dirs]
        hilo = [jnp.concatenate([h1_refs[d][rows[d], :], h2_refs[d][rows[d], :]], axis=0) for d in dirs]
        outs, sts = _hgrn_chunks([q_refs[d][rows[d], :] for d in dirs], [k_refs[d][rows[d], :] for d in dirs],
                                 [v_refs[d][rows[d], :] for d in dirs], hilo,
                                 [[st_ref[d, hd] for hd in range(SCAN_HEADS)] for d in dirs],
                                 [rs_ref[d] for d in dirs], [lv_ref[d] for d in dirs], xb_ref)
        for d in dirs:
            for hd in range(SCAN_HEADS):
                o_refs[d][rows[d], hd * A_HEAD_DIM:(hd + 1) * A_HEAD_DIM] = outs[d][hd].astype(BF16)
                st_ref[d, hd] = sts[d][hd]
        return carry

    lax.fori_loop(0, n_chunks, step, 0, unroll=2)


def _hgrn(zb, batch, seq):
    nt = seq // SCAN_TILE
    width = SCAN_HEADS * A_HEAD_DIM
    hw = A_HEADS // SCAN_HEADS
    rs_tab, lv_tab = _scan_tables()

    def fwd(group):
        return pl.BlockSpec((SCAN_TILE, width), lambda b, h, i: (b * nt + i, group * hw + h))

    def bwd(group):
        return pl.BlockSpec((SCAN_TILE, width), lambda b, h, i: (b * nt + nt - 1 - i, group * hw + h))

    out = jax.ShapeDtypeStruct((batch * seq, A_WIDTH), BF16)
    state = (2, SCAN_HEADS, A_HEAD_DIM, A_HEAD_DIM)
    return pl.pallas_call(
        _hgrn_body,
        grid=(batch, hw, nt),
        in_specs=[fwd(ZB_Q), fwd(ZB_H1F), fwd(ZB_H2F), fwd(ZB_KF), fwd(ZB_V),
                  bwd(ZB_Q), bwd(ZB_H1B), bwd(ZB_H2B), bwd(ZB_KB), bwd(ZB_V),
                  _resident(rs_tab.shape), _resident(lv_tab.shape)],
        out_specs=[pl.BlockSpec((SCAN_TILE, width), lambda b, h, i: (b * nt + i, h)),
                   pl.BlockSpec((SCAN_TILE, width), lambda b, h, i: (b * nt + nt - 1 - i, h))],
        out_shape=[out, out],
        scratch_shapes=[pltpu.VMEM(state, F32), pltpu.VMEM((2, SCAN_CHUNK, width), F32)],
        compiler_params=_params("parallel", "parallel", "arbitrary"),
        name="hgrn2_scan",
    )(*([zb] * 10), rs_tab, lv_tab)


HALF = LANE // 2


def _attn_bias(kvh, second):
    blk = ATTN_BLOCK
    t = lax.broadcasted_iota(jnp.int32, (2 * blk, 3 * blk), 0)
    s = lax.broadcasted_iota(jnp.int32, (2 * blk, 3 * blk), 1)
    upper = t >= blk
    dist = jnp.abs(jnp.where(upper, t - blk, t) - s + blk)
    head_lo = kvh * B_GROUP + second
    slope_lo = LOG2E * 2.0 ** (-(8.0 / B_HEADS) * (head_lo + 1))
    slope_hi = LOG2E * 2.0 ** (-(8.0 / B_HEADS) * (head_lo + 3))
    slope = jnp.where(upper, slope_hi, slope_lo)
    return jnp.where(dist <= WINDOW, -slope * dist.astype(F32), -jnp.inf)


def _attn_body(sink_ref, q_ref, kc_ref, kp_ref, kn_ref, vc_ref, vp_ref, vn_ref, o_ref, kv_scr, bias_scr):
    i = pl.program_id(1)
    blk = ATTN_BLOCK

    @pl.when((pl.program_id(0) == 0) & (i == 0))
    def _():
        for kvh in range(B_KV_HEADS):
            for second in range(2):
                bias_scr[kvh, second] = _attn_bias(kvh, second)

    lane = lax.broadcasted_iota(jnp.int32, (ATTN_TILE + 2 * blk, LANE), 1)
    low = lane < HALF
    q_scale = LOG2E * B_HEAD_DIM ** -0.5
    for x, (p_ref, c_ref, n_ref) in enumerate(((kp_ref, kc_ref, kn_ref), (vp_ref, vc_ref, vn_ref))):
        full = jnp.concatenate([p_ref[...], c_ref[...], n_ref[...]], axis=0)
        if x == 0:
            full = full * q_scale
        swapped = pltpu.roll(full, HALF, 1)
        kv_scr[x, 0, 0] = jnp.where(low, full, 0.0).astype(BF16)
        kv_scr[x, 0, 1] = jnp.where(low, 0.0, swapped).astype(BF16)
        kv_scr[x, 1, 0] = jnp.where(low, swapped, 0.0).astype(BF16)
        kv_scr[x, 1, 1] = jnp.where(low, 0.0, full).astype(BF16)

    s_idx = lax.broadcasted_iota(jnp.int32, (2 * blk, 3 * blk), 1)
    key_lo = jnp.where(i == 0, blk, 0)
    key_hi = jnp.where(i == pl.num_programs(1) - 1, 2 * blk, 3 * blk)
    upper = lax.broadcasted_iota(jnp.int32, (2 * blk, 1), 0) >= blk
    out_low = lax.broadcasted_iota(jnp.int32, (2 * blk, LANE), 1) < HALF
    n_blocks = ATTN_TILE // blk
    for j in range(n_blocks):
        rows = slice(j * blk, (j + 1) * blk)
        keys = slice(j * blk, (j + 3) * blk)
        for kvh in range(B_KV_HEADS):
            c0 = kvh * B_GROUP * B_HEAD_DIM
            q2 = jnp.concatenate([q_ref[rows, c0:c0 + LANE], q_ref[rows, c0 + LANE:c0 + 2 * LANE]], axis=0)
            probs, denoms = [], []
            for second in range(2):
                sc = _dot_nt(q2, kv_scr[0, kvh, second, keys, :]) + bias_scr[kvh, second]
                if j == 0:
                    sc = jnp.where(s_idx >= key_lo, sc, -jnp.inf)
                if j == n_blocks - 1:
                    sc = jnp.where(s_idx < key_hi, sc, -jnp.inf)
                head = kvh * B_GROUP + second
                sink = LOG2E * jnp.where(upper, sink_ref[head + 2], sink_ref[head])
                m = jnp.maximum(jnp.max(sc, axis=-1, keepdims=True), sink)
                p = jnp.exp2(sc - m)
                denoms.append(jnp.sum(p, axis=-1, keepdims=True) + jnp.exp2(sink - m))
                probs.append(p.astype(BF16))
            o = _dot(probs[0], kv_scr[1, kvh, 0, keys, :]) + _dot(probs[1], kv_scr[1, kvh, 1, keys, :])
            o = (o / jnp.where(out_low, denoms[0], denoms[1])).astype(BF16)
            o_ref[rows, c0:c0 + LANE] = o[0:blk]
            o_ref[rows, c0 + LANE:c0 + 2 * LANE] = o[blk:]


def _attn(zb, zf, sink, batch, seq):
    nt = seq // ATTN_TILE
    per = ATTN_TILE // ATTN_BLOCK
    nblk = seq // ATTN_BLOCK
    k_col, v_col = 0, 1

    def cur(col):
        return pl.BlockSpec((ATTN_TILE, B_KV_WIDTH), lambda b, i: (b * nt + i, col))

    def prev(col):
        return pl.BlockSpec((ATTN_BLOCK, B_KV_WIDTH),
                            lambda b, i: (b * nblk + jnp.maximum(i * per - 1, 0), col))

    def nxt(col):
        return pl.BlockSpec((ATTN_BLOCK, B_KV_WIDTH),
                            lambda b, i: (b * nblk + jnp.minimum((i + 1) * per, nblk - 1), col))

    return pl.pallas_call(
        _attn_body,
        grid=(batch, nt),
        in_specs=[pl.BlockSpec(memory_space=pltpu.SMEM),
                  pl.BlockSpec((ATTN_TILE, B_WIDTH), lambda b, i: (b * nt + i, ZB_QATTN)),
                  cur(k_col), prev(k_col), nxt(k_col), cur(v_col), prev(v_col), nxt(v_col)],
        out_specs=pl.BlockSpec((ATTN_TILE, B_WIDTH), lambda b, i: (b * nt + i, 0)),
        out_shape=jax.ShapeDtypeStruct((batch * seq, B_WIDTH), BF16),
        scratch_shapes=[pltpu.VMEM((2, B_KV_HEADS, 2, ATTN_TILE + 2 * ATTN_BLOCK, LANE), BF16),
                        pltpu.VMEM((B_KV_HEADS, 2, 2 * ATTN_BLOCK, 3 * ATTN_BLOCK), F32)],
        compiler_params=_params("arbitrary", "arbitrary"),
        name="window_attn",
    )(sink, zb, zf, zf, zf, zf, zf, zf)


def _sgu_body(x_ref, win_ref, cg_ref, cb_ref, ws_ref, bs_ref, wout_ref, lg_ref, lb_ref, y_ref,
              u_scr, v_scr, h_scr):
    step = 2 * MXU_DIM
    n_blocks = FFN_TILE // ROW_BLOCK

    def rows_of(r):
        return slice(r * ROW_BLOCK, (r + 1) * ROW_BLOCK)

    def project(r):
        u_blk, v_blk = u_scr.at[r % 2], v_scr.at[r % 2]
        xb = x_ref[rows_of(r), :].astype(BF16)
        for c in range(2 * C_WIDTH // step):
            z = _dot(xb, win_ref[:, c * step:(c + 1) * step])
            z = 0.5 * z * (1.0 + lax.erf(z * (2.0 ** -0.5)))
            if c < C_WIDTH // step:
                u_blk[:, c * step:(c + 1) * step] = z
            else:
                v_blk[:, c * step - C_WIDTH:(c + 1) * step - C_WIDTH] = z

    def gate(r):
        u_blk, v_blk, h_blk = u_scr.at[r % 2], v_scr.at[r % 2], h_scr.at[r % 2]
        v_blk[...] = _layernorm(v_blk[...], cg_ref[...], cb_ref[...])
        for g in range(C_GROUPS):
            w_g = ws_ref[g]
            bias = bs_ref[:, g:g + 1]
            cols = slice(g * C_GROUP_DIM, (g + 1) * C_GROUP_DIM)
            for n in range(ROW_BLOCK // C_CHUNK):
                chunk = slice(n * C_CHUNK, (n + 1) * C_CHUNK)
                s = _dot(w_g, v_blk[chunk, cols].astype(BF16)) + bias
                h_blk[chunk, cols] = (u_blk[chunk, cols] * s).astype(BF16)
        return _dot(h_blk[...], wout_ref[...])

    project(0)
    project(1)
    for r in range(n_blocks):
        y = gate(r)
        if r + 2 < n_blocks:
            project(r + 2)
        y_ref[rows_of(r), :] = _layernorm(DEEPNORM_ALPHA * x_ref[rows_of(r), :] + y, lg_ref[...], lb_ref[...])


def _sgu(x, w_in, c_g, c_b, w_s, b_s_t, w_out, ln_g, ln_b):
    n = x.shape[0]
    tm = FFN_TILE
    return pl.pallas_call(
        _sgu_body,
        grid=(n // tm,),
        in_specs=[pl.BlockSpec((tm, D_MODEL), lambda i: (i, 0)),
                  _resident((D_MODEL, 2 * C_WIDTH)), _resident((1, C_WIDTH)), _resident((1, C_WIDTH)),
                  _resident((C_GROUPS, C_CHUNK, C_CHUNK)), _resident((C_CHUNK, C_GROUPS)),
                  _resident((C_WIDTH, D_MODEL)), _resident((1, D_MODEL)), _resident((1, D_MODEL))],
        out_specs=pl.BlockSpec((tm, D_MODEL), lambda i: (i, 0)),
        out_shape=jax.ShapeDtypeStruct((n, D_MODEL), F32),
        scratch_shapes=[pltpu.VMEM((2, ROW_BLOCK, C_WIDTH), F32), pltpu.VMEM((2, ROW_BLOCK, C_WIDTH), F32),
                        pltpu.VMEM((2, ROW_BLOCK, C_WIDTH), BF16)],
        compiler_params=_params("parallel"),
        name="sgu_ln",
    )(x, w_in, c_g, c_b, w_s, b_s_t, w_out, ln_g, ln_b)


def _swiglu(xb, wg_ref, wu_ref, wd_ref, h_blk):
    for c in range(FFN_DIM // MXU_DIM):
        sl = slice(c * MXU_DIM, (c + 1) * MXU_DIM)
        gate = _dot(xb, wg_ref[:, sl])
        up = _dot(xb, wu_ref[:, sl])
        h_blk[:, sl] = (gate * jax.nn.sigmoid(gate) * up).astype(BF16)
    return _dot(h_blk[...], wd_ref[...])


def _ffn_body(x_ref, wg_ref, wu_ref, wd_ref, lg_ref, lb_ref, y_ref, h_scr):
    for r in range(FFN_TILE // ROW_BLOCK):
        rows = slice(r * ROW_BLOCK, (r + 1) * ROW_BLOCK)
        x = x_ref[rows, :]
        y = _swiglu(x.astype(BF16), wg_ref, wu_ref, wd_ref, h_scr.at[r % 2])
        y_ref[rows, :] = _layernorm(DEEPNORM_ALPHA * x + y, lg_ref[...], lb_ref[...])


def _ffn(x, wg, wu, wd, ln_g, ln_b):
    n = x.shape[0]
    tm = FFN_TILE
    return pl.pallas_call(
        _ffn_body,
        grid=(n // tm,),
        in_specs=[pl.BlockSpec((tm, D_MODEL), lambda i: (i, 0)),
                  _resident((D_MODEL, FFN_DIM)), _resident((D_MODEL, FFN_DIM)), _resident((FFN_DIM, D_MODEL)),
                  _resident((1, D_MODEL)), _resident((1, D_MODEL))],
        out_specs=pl.BlockSpec((tm, D_MODEL), lambda i: (i, 0)),
        out_shape=jax.ShapeDtypeStruct((n, D_MODEL), F32),
        scratch_shapes=[pltpu.VMEM((2, ROW_BLOCK, FFN_DIM), BF16)],
        compiler_params=_params("parallel"),
        name="swiglu_ln",
    )(x, wg, wu, wd, ln_g, ln_b)


def _mixffn_body(x_ref, of_ref, ob_ref, g_ref, oattn_ref, ng_ref, wo_ref, mg_ref, mb_ref,
                 wg_ref, wu_ref, wd_ref, lg_ref, lb_ref, y_ref, h_scr, x1_scr):
    n_blocks = FFN_TILE // ROW_BLOCK

    def rows_of(r):
        return slice(r * ROW_BLOCK, (r + 1) * ROW_BLOCK)

    def mixer(r):
        rows = rows_of(r)
        o = of_ref[rows, :].astype(F32) + ob_ref[rows, :].astype(F32)
        parts = []
        for h in range(A_HEADS):
            sl = slice(h * A_HEAD_DIM, (h + 1) * A_HEAD_DIM)
            oh = o[:, sl]
            oh = oh * lax.rsqrt(jnp.mean(oh * oh, axis=-1, keepdims=True) + RMS_EPS) * ng_ref[...]
            gh = g_ref[rows, sl].astype(F32)
            parts.append((oh * (gh * jax.nn.sigmoid(gh))).astype(BF16))
        mixed = jnp.concatenate(parts + [oattn_ref[rows, :]], axis=-1)
        x1_scr[r % 2] = _layernorm(DEEPNORM_ALPHA * x_ref[rows, :] + _dot(mixed, wo_ref[...]),
                                   mg_ref[...], mb_ref[...])

    mixer(0)
    for r in range(n_blocks):
        x1 = x1_scr[r % 2]
        if r + 1 < n_blocks:
            mixer(r + 1)
        y = _swiglu(x1.astype(BF16), wg_ref, wu_ref, wd_ref, h_scr.at[r % 2])
        y_ref[rows_of(r), :] = _layernorm(DEEPNORM_ALPHA * x1 + y, lg_ref[...], lb_ref[...])


def _mixffn(x, o_f, o_b, zb, o_attn, norm_g, w_out, mix_g, mix_b, wg, wu, wd, ln_g, ln_b):
    n = x.shape[0]
    tm = FFN_TILE

    def half(col):
        return pl.BlockSpec((tm, A_WIDTH), lambda i: (i, col))

    return pl.pallas_call(
        _mixffn_body,
        grid=(n // tm,),
        in_specs=[pl.BlockSpec((tm, D_MODEL), lambda i: (i, 0)), half(0), half(0), half(ZB_G), half(0),
                  _resident((1, A_HEAD_DIM)), _resident((A_WIDTH + B_WIDTH, D_MODEL)),
                  _resident((1, D_MODEL)), _resident((1, D_MODEL)),
                  _resident((D_MODEL, FFN_DIM)), _resident((D_MODEL, FFN_DIM)), _resident((FFN_DIM, D_MODEL)),
                  _resident((1, D_MODEL)), _resident((1, D_MODEL))],
        out_specs=pl.BlockSpec((tm, D_MODEL), lambda i: (i, 0)),
        out_shape=jax.ShapeDtypeStruct((n, D_MODEL), F32),
        scratch_shapes=[pltpu.VMEM((2, ROW_BLOCK, FFN_DIM), BF16), pltpu.VMEM((2, ROW_BLOCK, D_MODEL), F32)],
        compiler_params=_params("parallel"),
        name="mix_swiglu_ln",
    )(x, o_f, o_b, zb, o_attn, norm_g, w_out, mix_g, mix_b, wg, wu, wd, ln_g, ln_b)


def _trunk(x, p):
    batch, seq, _ = x.shape
    x = x.reshape(batch * seq, D_MODEL)
    for layer in range(DEPTH):
        j = layer // 2
        ffn = (p["ffn_w_gate"][layer], p["ffn_w_up"][layer], p["ffn_w_down"][layer],
               p["ln_ffn_g"][layer], p["ln_ffn_b"][layer])
        if layer % 2 == 0:
            zb, zf = _inproj(x, p["w_in_ab"][j], p["gate_consts"][j])
            o_f, o_b = _hgrn(zb, batch, seq)
            o_attn = _attn(zb, zf, p["attn_sink"][j], batch, seq)
            x = _mixffn(x, o_f, o_b, zb, o_attn, p["hgrn_norm_g"][j], p["w_out_ab"][j],
                        p["ln_mix_g"][layer], p["ln_mix_b"][layer], *ffn)
        else:
            x = _sgu(x, p["w_in_c"][j], p["c_ln_g"][j], p["c_ln_b"][j], p["c_ws"][j], p["c_bs_t"][j],
                     p["w_out_c"][j], p["ln_mix_g"][layer], p["ln_mix_b"][layer])
            x = _ffn(x, *ffn)
    return x.reshape(batch, seq, D_MODEL)


def kernel(x_prompt, x_sample, w_in_ab, hgrn_lb_logits, hgrn_norm_g, attn_sink, w_out_ab, w_in_c, c_ln_g,
           c_ln_b, c_ws, c_bs, w_out_c, ffn_w_gate, ffn_w_up, ffn_w_down, ln_mix_g, ln_mix_b, ln_ffn_g,
           ln_ffn_b):
    prob = jax.nn.softmax(hgrn_lb_logits.astype(F32), axis=0)
    lower = jnp.maximum(jnp.cumsum(prob, axis=0) - prob[0:1], 0.0)
    gate_consts = jnp.stack([jnp.log(lower), jnp.log1p(-lower)], axis=2)
    row = lambda a: a.astype(F32)[:, None, :]
    p = dict(
        w_in_ab=w_in_ab.astype(BF16), gate_consts=gate_consts, hgrn_norm_g=row(hgrn_norm_g),
        attn_sink=attn_sink.astype(F32), w_out_ab=w_out_ab.astype(BF16),
        w_in_c=w_in_c.astype(BF16), c_ln_g=row(c_ln_g), c_ln_b=row(c_ln_b), c_ws=c_ws.astype(BF16),
        c_bs_t=jnp.swapaxes(c_bs.astype(F32), 1, 2), w_out_c=w_out_c.astype(BF16),
        ffn_w_gate=ffn_w_gate.astype(BF16), ffn_w_up=ffn_w_up.astype(BF16), ffn_w_down=ffn_w_down.astype(BF16),
        ln_mix_g=row(ln_mix_g), ln_mix_b=row(ln_mix_b), ln_ffn_g=row(ln_ffn_g), ln_ffn_b=row(ln_ffn_b),
    )
    return (_trunk(x_prompt, p), _trunk(x_sample, p))
```

```python
import math

import numpy as np
import jax
import jax.numpy as jnp
from jax import lax
from jax.experimental import pallas as pl
from jax.experimental.pallas import tpu as pltpu

D_MODEL = 1024
DEPTH = 4
A_HEADS = 4
A_HEAD_DIM = 128
A_WIDTH = A_HEADS * A_HEAD_DIM
B_HEADS = 8
B_KV_HEADS = 2
B_GROUP = B_HEADS // B_KV_HEADS
B_HEAD_DIM = 64
B_WIDTH = B_HEADS * B_HEAD_DIM
B_KV_WIDTH = B_KV_HEADS * B_HEAD_DIM
WINDOW = 128
ATTN_BLOCK = 128
AB_IN = 5 * A_WIDTH + B_WIDTH + 2 * B_KV_WIDTH
C_WIDTH = 2 * D_MODEL
C_GROUPS = 8
C_GROUP_DIM = C_WIDTH // C_GROUPS
C_CHUNK = 128
FFN_DIM = 2816
DEEPNORM_ALPHA = (2.0 * DEPTH) ** 0.25
LN_EPS = 1e-5
RMS_EPS = 1e-6

LANE = 128
MXU_DIM = 256
VMEM_LIMIT = 56 * 1024 * 1024

TOKEN_TILE = 512
FFN_TILE = 1024
ROW_BLOCK = 256
SCAN_TILE = 1024
SCAN_CHUNK = 128
ATTN_TILE = 1024

F32 = jnp.float32
BF16 = jnp.bfloat16
LOG2E = math.log2(math.e)


def _dot(a, b):
    return jnp.dot(a, b, preferred_element_type=F32)


def _dot_nt(a, b):
    return lax.dot_general(a, b, (((1,), (1,)), ((), ())), preferred_element_type=F32)


def _dot_tn(a, b):
    return lax.dot_general(a, b, (((0,), (0,)), ((), ())), preferred_element_type=F32)


def _layernorm(x, g, b):
    mu = jnp.mean(x, axis=-1, keepdims=True)
    xc = x - mu
    var = jnp.mean(xc * xc, axis=-1, keepdims=True)
    return xc * lax.rsqrt(var + LN_EPS) * g + b


def _resident(shape):
    nd = len(shape)
    return pl.BlockSpec(shape, lambda *_: (0,) * nd, pipeline_mode=pl.Buffered(1))


def _params(*sem):
    return pltpu.CompilerParams(dimension_semantics=sem, vmem_limit_bytes=VMEM_LIMIT)


ZB_Q, ZB_H1F, ZB_H2F, ZB_KF, ZB_H1B, ZB_H2B, ZB_KB, ZB_V, ZB_G, ZB_QATTN = range(10)
ZB_WIDTH = 10 * A_WIDTH
assert B_WIDTH == A_WIDTH
ZF_WIDTH = 2 * B_KV_WIDTH


def _forget_gate(x, log_lb, l1m):
    log_sig = jnp.minimum(x, 0.0) - jnp.log(1.0 + jnp.exp(-jnp.abs(x)))
    c = l1m + log_sig
    logf = jnp.maximum(log_lb, c) + jnp.log(1.0 + jnp.exp(-jnp.abs(log_lb - c)))
    lf2 = logf * LOG2E
    hi = lf2.astype(BF16)
    lo = (lf2 - hi.astype(F32)).astype(BF16)
    k = jnp.exp(c - x)
    return hi, lo, k.astype(BF16)


def _inproj_body(x_ref, w_ref, gc_ref, zb_ref, zf_ref):
    step = MXU_DIM
    per_group = A_WIDTH // step
    n_chunks = AB_IN // step
    gate_chunks = list(range(per_group, 3 * per_group))
    other_chunks = [c for c in range(n_chunks) if c not in gate_chunks]
    order = []
    while gate_chunks or other_chunks:
        order += gate_chunks[:1] + other_chunks[:2]
        gate_chunks, other_chunks = gate_chunks[1:], other_chunks[2:]
    xb = x_ref[...].astype(BF16)
    for c in order:
        z = _dot(xb, w_ref[:, c * step:(c + 1) * step])
        group, part = divmod(c, per_group)
        sub = slice(part * step, (part + 1) * step)

        def put(zb_group, val):
            zb_ref[:, zb_group * A_WIDTH + part * step:zb_group * A_WIDTH + (part + 1) * step] = val

        if group == 0:
            put(ZB_Q, (z * jax.nn.sigmoid(z)).astype(BF16))
        elif group in (1, 2):
            d = group - 1
            hi, lo, k = _forget_gate(z, gc_ref[d, 0:1, sub], gc_ref[d, 1:2, sub])
            put(ZB_H1F if d == 0 else ZB_H1B, hi)
            put(ZB_H2F if d == 0 else ZB_H2B, lo)
            put(ZB_KF if d == 0 else ZB_KB, k)
        elif group in (3, 4, 5):
            put({3: ZB_V, 4: ZB_G, 5: ZB_QATTN}[group], z.astype(BF16))
        else:
            zf_ref[...] = z


def _inproj(x, w, gate_consts):
    n = x.shape[0]
    tm = TOKEN_TILE
    return pl.pallas_call(
        _inproj_body,
        grid=(n // tm,),
        in_specs=[pl.BlockSpec((tm, D_MODEL), lambda i: (i, 0)), _resident((D_MODEL, AB_IN)),
                  _resident((2, 2, A_WIDTH))],
        out_specs=[pl.BlockSpec((tm, ZB_WIDTH), lambda i: (i, 0)), pl.BlockSpec((tm, ZF_WIDTH), lambda i: (i, 0))],
        out_shape=[jax.ShapeDtypeStruct((n, ZB_WIDTH), BF16), jax.ShapeDtypeStruct((n, ZF_WIDTH), F32)],
        compiler_params=_params("parallel"),
        name="inproj",
    )(x, w, gate_consts)


N_LEVELS = SCAN_CHUNK.bit_length() - 1
SUBLANES = 8
SCAN_HEADS = 4
MXU_LEVELS = 2


def _level_matrix(lvl):
    c = SCAN_CHUNK
    h = c >> lvl
    t = np.arange(c)[:, None]
    u = np.arange(c)[None, :]
    boundary = (t // (2 * h)) * 2 * h + h - 1
    second = (t % (2 * h)) >= h
    return np.where(second, (u > boundary) & (u <= t), (u > t) & (u <= boundary))


def _scan_tables():
    c = SCAN_CHUNK
    t = np.arange(c)[:, None]
    u = np.arange(c)[None, :]
    level = np.where(u == t, 0, -1)
    for lvl in range(1, N_LEVELS + 1):
        h = c >> lvl
        level = np.where((u < t) & ((t ^ u) >= h) & ((t ^ u) < 2 * h), lvl, level)
    mats = [u <= t] + [_level_matrix(lvl) for lvl in range(N_LEVELS - MXU_LEVELS + 1, N_LEVELS + 1)]
    fwd = np.concatenate(mats, axis=0)
    bwd = np.concatenate([m[::-1, ::-1] for m in mats], axis=0)
    rs = np.stack([fwd, bwd]).astype(np.float32)
    rs2 = jnp.asarray(np.concatenate([rs, rs], axis=2), dtype=BF16)
    lv = jnp.asarray(np.stack([level, level.T]).astype(np.int32))
    return rs2, lv


def _neg_abs(x):
    bits = lax.bitcast_convert_type(x, jnp.uint32) | jnp.uint32(0x80000000)
    return lax.bitcast_convert_type(bits, F32)


def _level_exponents(xb, xb_ref, lvl, rev):
    cs = SCAN_CHUNK
    h = cs >> lvl
    assert 2 * h >= SUBLANES
    pieces = []
    for start in range(0, cs, 2 * h):
        bnd = start + (h if rev else h - 1)
        pieces.append(_neg_abs(xb[start:start + 2 * h] - xb_ref[bnd:bnd + 1, :]))
    return jnp.concatenate(pieces, axis=0)


def _hgrn_chunks(q, k, v, hilo, sts, rs2, level, xb_ref):
    cs = SCAN_CHUNK
    dirs = (0, 1)
    heads = [slice(hd * A_HEAD_DIM, (hd + 1) * A_HEAD_DIM) for hd in range(SCAN_HEADS)]
    x = [_dot(rs2[d], hilo[d]) for d in dirs]
    xb = [x[d][0:cs] for d in dirs]
    for d in dirs:
        xb_ref[d] = xb[d]
    x_last = [xb_ref[d, (0 if d else cs - 1):(1 if d else cs), :] for d in dirs]

    k_t = [[k[d][:, hs].T for hs in heads] for d in dirs]
    a = [[jnp.where(level[d] == 0, _dot(q[d][:, hs], kt), 0.0) for hs, kt in zip(heads, k_t[d])] for d in dirs]
    for lvl in range(1, N_LEVELS + 1):
        from_mxu = lvl - (N_LEVELS - MXU_LEVELS)
        for d in dirs:
            if from_mxu >= 1:
                e = x[d][from_mxu * cs:(from_mxu + 1) * cs]
            else:
                e = _level_exponents(xb[d], xb_ref.at[d], lvl, bool(d))
            p = jnp.exp2(e).astype(BF16)
            qp = q[d] * p
            a[d] = [jnp.where(level[d] == lvl, _dot(qp[:, hs], kt * p[:, hs].T), a_h)
                    for hs, kt, a_h in zip(heads, k_t[d], a[d])]

    outs, new_sts = [], []
    for d in dirs:
        q_st = q[d] * jnp.exp2(xb[d]).astype(BF16)
        k_st = k[d] * jnp.exp2(_neg_abs(xb[d] - x_last[d])).astype(BF16)
        decay = jnp.exp2(x_last[d])
        outs.append([_dot(a_h.astype(BF16), v[d][:, hs]) + _dot_nt(q_st[:, hs], st.astype(BF16))
                     for hs, a_h, st in zip(heads, a[d], sts[d])])
        new_sts.append([st * decay[:, hs] + _dot_tn(v[d][:, hs], k_st[:, hs]) for hs, st in zip(heads, sts[d])])
    return outs, new_sts


def _hgrn_body(qf_ref, h1f_ref, h2f_ref, kf_ref, vf_ref, qb_ref, h1b_ref, h2b_ref, kb_ref, vb_ref,
               rs_ref, lv_ref, of_ref, ob_ref, st_ref, xb_ref):
    @pl.when(pl.program_id(2) == 0)
    def _():
        st_ref[...] = jnp.zeros_like(st_ref)

    cs = SCAN_CHUNK
    n_chunks = SCAN_TILE // cs
    dirs = (0, 1)
    q_refs, h1_refs, h2_refs = (qf_ref, qb_ref), (h1f_ref, h1b_ref), (h2f_ref, h2b_ref)
    k_refs, v_refs, o_refs = (kf_ref, kb_ref), (vf_ref, vb_ref), (of_ref, ob_ref)

    def step(j, carry):
        rows = [pl.ds(pl.multiple_of((n_chunks - 1 - j if d else j) * cs, cs), cs) for d in dirs]
        hilo = [jnp.concatenate([h1_refs[d][rows[d], :], h2_refs[d][rows[d], :]], axis=0) for d in dirs]
        outs, sts = _hgrn_chunks([q_refs[d][rows[d], :] for d in dirs], [k_refs[d][rows[d], :] for d in dirs],
                                 [v_refs[d][rows[d], :] for d in dirs], hilo,
                                 [[st_ref[d, hd] for hd in range(SCAN_HEADS)] for d in dirs],
                                 [rs_ref[d] for d in dirs], [lv_ref[d] for d in dirs], xb_ref)
        for d in dirs:
            for hd in range(SCAN_HEADS):
                o_refs[d][rows[d], hd * A_HEAD_DIM:(hd + 1) * A_HEAD_DIM] = outs[d][hd].astype(BF16)
                st_ref[d, hd] = sts[d][hd]
        return carry

    lax.fori_loop(0, n_chunks, step, 0, unroll=2)


def _hgrn(zb, batch, seq):
    nt = seq // SCAN_TILE
    width = SCAN_HEADS * A_HEAD_DIM
    hw = A_HEADS // SCAN_HEADS
    rs_tab, lv_tab = _scan_tables()

    def fwd(group):
        return pl.BlockSpec((SCAN_TILE, width), lambda b, h, i: (b * nt + i, group * hw + h))

    def bwd(group):
        return pl.BlockSpec((SCAN_TILE, width), lambda b, h, i: (b * nt + nt - 1 - i, group * hw + h))

    out = jax.ShapeDtypeStruct((batch * seq, A_WIDTH), BF16)
    state = (2, SCAN_HEADS, A_HEAD_DIM, A_HEAD_DIM)
    return pl.pallas_call(
        _hgrn_body,
        grid=(batch, hw, nt),
        in_specs=[fwd(ZB_Q), fwd(ZB_H1F), fwd(ZB_H2F), fwd(ZB_KF), fwd(ZB_V),
                  bwd(ZB_Q), bwd(ZB_H1B), bwd(ZB_H2B), bwd(ZB_KB), bwd(ZB_V),
                  _resident(rs_tab.shape), _resident(lv_tab.shape)],
        out_specs=[pl.BlockSpec((SCAN_TILE, width), lambda b, h, i: (b * nt + i, h)),
                   pl.BlockSpec((SCAN_TILE, width), lambda b, h, i: (b * nt + nt - 1 - i, h))],
        out_shape=[out, out],
        scratch_shapes=[pltpu.VMEM(state, F32), pltpu.VMEM((2, SCAN_CHUNK, width), F32)],
        compiler_params=_params("parallel", "parallel", "arbitrary"),
        name="hgrn2_scan",
    )(*([zb] * 10), rs_tab, lv_tab)


HALF = LANE // 2


def _attn_bias(kvh, second):
    blk = ATTN_BLOCK
    t = lax.broadcasted_iota(jnp.int32, (2 * blk, 3 * blk), 0)
    s = lax.broadcasted_iota(jnp.int32, (2 * blk, 3 * blk), 1)
    upper = t >= blk
    dist = jnp.abs(jnp.where(upper, t - blk, t) - s + blk)
    head_lo = kvh * B_GROUP + second
    slope_lo = LOG2E * 2.0 ** (-(8.0 / B_HEADS) * (head_lo + 1))
    slope_hi = LOG2E * 2.0 ** (-(8.0 / B_HEADS) * (head_lo + 3))
    slope = jnp.where(upper, slope_hi, slope_lo)
    return jnp.where(dist <= WINDOW, -slope * dist.astype(F32), -jnp.inf)


def _attn_body(sink_ref, q_ref, kc_ref, kp_ref, kn_ref, vc_ref, vp_ref, vn_ref, o_ref, kv_scr, bias_scr):
    i = pl.program_id(1)
    blk = ATTN_BLOCK

    @pl.when((pl.program_id(0) == 0) & (i == 0))
    def _():
        for kvh in range(B_KV_HEADS):
            for second in range(2):
                bias_scr[kvh, second] = _attn_bias(kvh, second)

    lane = lax.broadcasted_iota(jnp.int32, (ATTN_TILE + 2 * blk, LANE), 1)
    low = lane < HALF
    q_scale = LOG2E * B_HEAD_DIM ** -0.5
    for x, (p_ref, c_ref, n_ref) in enumerate(((kp_ref, kc_ref, kn_ref), (vp_ref, vc_ref, vn_ref))):
        full = jnp.concatenate([p_ref[...], c_ref[...], n_ref[...]], axis=0)
        if x == 0:
            full = full * q_scale
        swapped = pltpu.roll(full, HALF, 1)
        kv_scr[x, 0, 0] = jnp.where(low, full, 0.0).astype(BF16)
        kv_scr[x, 0, 1] = jnp.where(low, 0.0, swapped).astype(BF16)
        kv_scr[x, 1, 0] = jnp.where(low, swapped, 0.0).astype(BF16)
        kv_scr[x, 1, 1] = jnp.where(low, 0.0, full).astype(BF16)

    s_idx = lax.broadcasted_iota(jnp.int32, (2 * blk, 3 * blk), 1)
    key_lo = jnp.where(i == 0, blk, 0)
    key_hi = jnp.where(i == pl.num_programs(1) - 1, 2 * blk, 3 * blk)
    upper = lax.broadcasted_iota(jnp.int32, (2 * blk, 1), 0) >= blk
    out_low = lax.broadcasted_iota(jnp.int32, (2 * blk, LANE), 1) < HALF
    n_blocks = ATTN_TILE // blk
    units = [(kvh, second) for kvh in range(B_KV_HEADS) for second in range(2)]
    sinks = [LOG2E * jnp.where(upper, sink_ref[kvh * B_GROUP + second + 2], sink_ref[kvh * B_GROUP + second])
             for kvh, second in units]
    for j in range(n_blocks):
        rows = slice(j * blk, (j + 1) * blk)
        keys = slice(j * blk, (j + 3) * blk)
        scores = []
        for kvh, second in units:
            c0 = kvh * B_GROUP * B_HEAD_DIM
            q2 = jnp.concatenate([q_ref[rows, c0:c0 + LANE], q_ref[rows, c0 + LANE:c0 + 2 * LANE]], axis=0)
            sc = _dot_nt(q2, kv_scr[0, kvh, second, keys, :]) + bias_scr[kvh, second]
            if j == 0:
                sc = jnp.where(s_idx >= key_lo, sc, -jnp.inf)
            if j == n_blocks - 1:
                sc = jnp.where(s_idx < key_hi, sc, -jnp.inf)
            scores.append(sc)
        maxes = [jnp.maximum(jnp.max(sc, axis=-1, keepdims=True), sink) for sc, sink in zip(scores, sinks)]
        probs = [jnp.exp2(sc - m) for sc, m in zip(scores, maxes)]
        denoms = [jnp.sum(p, axis=-1, keepdims=True) + jnp.exp2(sink - m)
                  for p, sink, m in zip(probs, sinks, maxes)]
        probs = [p.astype(BF16) for p in probs]
        for kvh in range(B_KV_HEADS):
            c0 = kvh * B_GROUP * B_HEAD_DIM
            lo, hi = 2 * kvh, 2 * kvh + 1
            o = _dot(probs[lo], kv_scr[1, kvh, 0, keys, :]) + _dot(probs[hi], kv_scr[1, kvh, 1, keys, :])
            o = (o / jnp.where(out_low, denoms[lo], denoms[hi])).astype(BF16)
            o_ref[rows, c0:c0 + LANE] = o[0:blk]
            o_ref[rows, c0 + LANE:c0 + 2 * LANE] = o[blk:]


def _attn(zb, zf, sink, batch, seq):
    nt = seq // ATTN_TILE
    per = ATTN_TILE // ATTN_BLOCK
    nblk = seq // ATTN_BLOCK
    k_col, v_col = 0, 1

    def cur(col):
        return pl.BlockSpec((ATTN_TILE, B_KV_WIDTH), lambda b, i: (b * nt + i, col))

    def prev(col):
        return pl.BlockSpec((ATTN_BLOCK, B_KV_WIDTH),
                            lambda b, i: (b * nblk + jnp.maximum(i * per - 1, 0), col))

    def nxt(col):
        return pl.BlockSpec((ATTN_BLOCK, B_KV_WIDTH),
                            lambda b, i: (b * nblk + jnp.minimum((i + 1) * per, nblk - 1), col))

    return pl.pallas_call(
        _attn_body,
        grid=(batch, nt),
        in_specs=[pl.BlockSpec(memory_space=pltpu.SMEM),
                  pl.BlockSpec((ATTN_TILE, B_WIDTH), lambda b, i: (b * nt + i, ZB_QATTN)),
                  cur(k_col), prev(k_col), nxt(k_col), cur(v_col), prev(v_col), nxt(v_col)],
        out_specs=pl.BlockSpec((ATTN_TILE, B_WIDTH), lambda b, i: (b * nt + i, 0)),
        out_shape=jax.ShapeDtypeStruct((batch * seq, B_WIDTH), BF16),
        scratch_shapes=[pltpu.VMEM((2, B_KV_HEADS, 2, ATTN_TILE + 2 * ATTN_BLOCK, LANE), BF16),
                        pltpu.VMEM((B_KV_HEADS, 2, 2 * ATTN_BLOCK, 3 * ATTN_BLOCK), F32)],
        compiler_params=_params("arbitrary", "arbitrary"),
        name="window_attn",
    )(sink, zb, zf, zf, zf, zf, zf, zf)


def _sgu_body(x_ref, win_ref, cg_ref, cb_ref, ws_ref, bs_ref, wout_ref, lg_ref, lb_ref, y_ref,
              u_scr, v_scr, h_scr):
    step = 2 * MXU_DIM
    n_blocks = FFN_TILE // ROW_BLOCK

    def rows_of(r):
        return slice(r * ROW_BLOCK, (r + 1) * ROW_BLOCK)

    def project(r):
        u_blk, v_blk = u_scr.at[r % 2], v_scr.at[r % 2]
        xb = x_ref[rows_of(r), :].astype(BF16)
        for c in range(2 * C_WIDTH // step):
            z = _dot(xb, win_ref[:, c * step:(c + 1) * step])
            z = 0.5 * z * (1.0 + lax.erf(z * (2.0 ** -0.5)))
            if c < C_WIDTH // step:
                u_blk[:, c * step:(c + 1) * step] = z
            else:
                v_blk[:, c * step - C_WIDTH:(c + 1) * step - C_WIDTH] = z

    def gate(r):
        u_blk, v_blk, h_blk = u_scr.at[r % 2], v_scr.at[r % 2], h_scr.at[r % 2]
        v_blk[...] = _layernorm(v_blk[...], cg_ref[...], cb_ref[...])
        for g in range(C_GROUPS):
            w_g = ws_ref[g]
            bias = bs_ref[:, g:g + 1]
            cols = slice(g * C_GROUP_DIM, (g + 1) * C_GROUP_DIM)
            for n in range(ROW_BLOCK // C_CHUNK):
                chunk = slice(n * C_CHUNK, (n + 1) * C_CHUNK)
                s = _dot(w_g, v_blk[chunk, cols].astype(BF16)) + bias
                h_blk[chunk, cols] = (u_blk[chunk, cols] * s).astype(BF16)
        return _dot(h_blk[...], wout_ref[...])

    project(0)
    project(1)
    for r in range(n_blocks):
        y = gate(r)
        if r + 2 < n_blocks:
            project(r + 2)
        y_ref[rows_of(r), :] = _layernorm(DEEPNORM_ALPHA * x_ref[rows_of(r), :] + y, lg_ref[...], lb_ref[...])


def _sgu(x, w_in, c_g, c_b, w_s, b_s_t, w_out, ln_g, ln_b):
    n = x.shape[0]
    tm = FFN_TILE
    return pl.pallas_call(
        _sgu_body,
        grid=(n // tm,),
        in_specs=[pl.BlockSpec((tm, D_MODEL), lambda i: (i, 0)),
                  _resident((D_MODEL, 2 * C_WIDTH)), _resident((1, C_WIDTH)), _resident((1, C_WIDTH)),
                  _resident((C_GROUPS, C_CHUNK, C_CHUNK)), _resident((C_CHUNK, C_GROUPS)),
                  _resident((C_WIDTH, D_MODEL)), _resident((1, D_MODEL)), _resident((1, D_MODEL))],
        out_specs=pl.BlockSpec((tm, D_MODEL), lambda i: (i, 0)),
        out_shape=jax.ShapeDtypeStruct((n, D_MODEL), F32),
        scratch_shapes=[pltpu.VMEM((2, ROW_BLOCK, C_WIDTH), F32), pltpu.VMEM((2, ROW_BLOCK, C_WIDTH), F32),
                        pltpu.VMEM((2, ROW_BLOCK, C_WIDTH), BF16)],
        compiler_params=_params("parallel"),
        name="sgu_ln",
    )(x, w_in, c_g, c_b, w_s, b_s_t, w_out, ln_g, ln_b)


def _swiglu(xb, wg_ref, wu_ref, wd_ref, h_blk):
    for c in range(FFN_DIM // MXU_DIM):
        sl = slice(c * MXU_DIM, (c + 1) * MXU_DIM)
        gate = _dot(xb, wg_ref[:, sl])
        up = _dot(xb, wu_ref[:, sl])
        h_blk[:, sl] = (gate * jax.nn.sigmoid(gate) * up).astype(BF16)
    return _dot(h_blk[...], wd_ref[...])


def _ffn_body(x_ref, wg_ref, wu_ref, wd_ref, lg_ref, lb_ref, y_ref, h_scr):
    for r in range(FFN_TILE // ROW_BLOCK):
        rows = slice(r * ROW_BLOCK, (r + 1) * ROW_BLOCK)
        x = x_ref[rows, :]
        y = _swiglu(x.astype(BF16), wg_ref, wu_ref, wd_ref, h_scr.at[r % 2])
        y_ref[rows, :] = _layernorm(DEEPNORM_ALPHA * x + y, lg_ref[...], lb_ref[...])


def _ffn(x, wg, wu, wd, ln_g, ln_b):
    n = x.shape[0]
    tm = FFN_TILE
    return pl.pallas_call(
        _ffn_body,
        grid=(n // tm,),
        in_specs=[pl.BlockSpec((tm, D_MODEL), lambda i: (i, 0)),
                  _resident((D_MODEL, FFN_DIM)), _resident((D_MODEL, FFN_DIM)), _resident((FFN_DIM, D_MODEL)),
                  _resident((1, D_MODEL)), _resident((1, D_MODEL))],
        out_specs=pl.BlockSpec((tm, D_MODEL), lambda i: (i, 0)),
        out_shape=jax.ShapeDtypeStruct((n, D_MODEL), F32),
        scratch_shapes=[pltpu.VMEM((2, ROW_BLOCK, FFN_DIM), BF16)],
        compiler_params=_params("parallel"),
        name="swiglu_ln",
    )(x, wg, wu, wd, ln_g, ln_b)


def _mixffn_body(x_ref, of_ref, ob_ref, g_ref, oattn_ref, ng_ref, wo_ref, mg_ref, mb_ref,
                 wg_ref, wu_ref, wd_ref, lg_ref, lb_ref, y_ref, h_scr, x1_scr):
    n_blocks = FFN_TILE // ROW_BLOCK

    def rows_of(r):
        return slice(r * ROW_BLOCK, (r + 1) * ROW_BLOCK)

    def mixer(r):
        rows = rows_of(r)
        o = of_ref[rows, :].astype(F32) + ob_ref[rows, :].astype(F32)
        parts = []
        for h in range(A_HEADS):
            sl = slice(h * A_HEAD_DIM, (h + 1) * A_HEAD_DIM)
            oh = o[:, sl]
            oh = oh * lax.rsqrt(jnp.mean(oh * oh, axis=-1, keepdims=True) + RMS_EPS) * ng_ref[...]
            gh = g_ref[rows, sl].astype(F32)
            parts.append((oh * (gh * jax.nn.sigmoid(gh))).astype(BF16))
        mixed = jnp.concatenate(parts + [oattn_ref[rows, :]], axis=-1)
        x1_scr[r % 2] = _layernorm(DEEPNORM_ALPHA * x_ref[rows, :] + _dot(mixed, wo_ref[...]),
                                   mg_ref[...], mb_ref[...])

    mixer(0)
    for r in range(n_blocks):
        x1 = x1_scr[r % 2]
        if r + 1 < n_blocks:
            mixer(r + 1)
        y = _swiglu(x1.astype(BF16), wg_ref, wu_ref, wd_ref, h_scr.at[r % 2])
        y_ref[rows_of(r), :] = _layernorm(DEEPNORM_ALPHA * x1 + y, lg_ref[...], lb_ref[...])


def _mixffn(x, o_f, o_b, zb, o_attn, norm_g, w_out, mix_g, mix_b, wg, wu, wd, ln_g, ln_b):
    n = x.shape[0]
    tm = FFN_TILE

    def half(col):
        return pl.BlockSpec((tm, A_WIDTH), lambda i: (i, col))

    return pl.pallas_call(
        _mixffn_body,
        grid=(n // tm,),
        in_specs=[pl.BlockSpec((tm, D_MODEL), lambda i: (i, 0)), half(0), half(0), half(ZB_G), half(0),
                  _resident((1, A_HEAD_DIM)), _resident((A_WIDTH + B_WIDTH, D_MODEL)),
                  _resident((1, D_MODEL)), _resident((1, D_MODEL)),
                  _resident((D_MODEL, FFN_DIM)), _resident((D_MODEL, FFN_DIM)), _resident((FFN_DIM, D_MODEL)),
                  _resident((1, D_MODEL)), _resident((1, D_MODEL))],
        out_specs=pl.BlockSpec((tm, D_MODEL), lambda i: (i, 0)),
        out_shape=jax.ShapeDtypeStruct((n, D_MODEL), F32),
        scratch_shapes=[pltpu.VMEM((2, ROW_BLOCK, FFN_DIM), BF16), pltpu.VMEM((2, ROW_BLOCK, D_MODEL), F32)],
        compiler_params=_params("parallel"),
        name="mix_swiglu_ln",
    )(x, o_f, o_b, zb, o_attn, norm_g, w_out, mix_g, mix_b, wg, wu, wd, ln_g, ln_b)


def _trunk(x, p):
    batch, seq, _ = x.shape
    x = x.reshape(batch * seq, D_MODEL)
    for layer in range(DEPTH):
        j = layer // 2
        ffn = (p["ffn_w_gate"][layer], p["ffn_w_up"][layer], p["ffn_w_down"][layer],
               p["ln_ffn_g"][layer], p["ln_ffn_b"][layer])
        if layer % 2 == 0:
            zb, zf = _inproj(x, p["w_in_ab"][j], p["gate_consts"][j])
            o_f, o_b = _hgrn(zb, batch, seq)
            o_attn = _attn(zb, zf, p["attn_sink"][j], batch, seq)
            x = _mixffn(x, o_f, o_b, zb, o_attn, p["hgrn_norm_g"][j], p["w_out_ab"][j],
                        p["ln_mix_g"][layer], p["ln_mix_b"][layer], *ffn)
        else:
            x = _sgu(x, p["w_in_c"][j], p["c_ln_g"][j], p["c_ln_b"][j], p["c_ws"][j], p["c_bs_t"][j],
                     p["w_out_c"][j], p["ln_mix_g"][layer], p["ln_mix_b"][layer])
            x = _ffn(x, *ffn)
    return x.reshape(batch, seq, D_MODEL)


def kernel(x_prompt, x_sample, w_in_ab, hgrn_lb_logits, hgrn_norm_g, attn_sink, w_out_ab, w_in_c, c_ln_g,
           c_ln_b, c_ws, c_bs, w_out_c, ffn_w_gate, ffn_w_up, ffn_w_down, ln_mix_g, ln_mix_b, ln_ffn_g,
           ln_ffn_b):
    prob = jax.nn.softmax(hgrn_lb_logits.astype(F32), axis=0)
    lower = jnp.maximum(jnp.cumsum(prob, axis=0) - prob[0:1], 0.0)
    gate_consts = jnp.stack([jnp.log(lower), jnp.log1p(-lower)], axis=2)
    row = lambda a: a.astype(F32)[:, None, :]
    p = dict(
        w_in_ab=w_in_ab.astype(BF16), gate_consts=gate_consts, hgrn_norm_g=row(hgrn_norm_g),
        attn_sink=attn_sink.astype(F32), w_out_ab=w_out_ab.astype(BF16),
        w_in_c=w_in_c.astype(BF16), c_ln_g=row(c_ln_g), c_ln_b=row(c_ln_b), c_ws=c_ws.astype(BF16),
        c_bs_t=jnp.swapaxes(c_bs.astype(F32), 1, 2), w_out_c=w_out_c.astype(BF16),
        ffn_w_gate=ffn_w_gate.astype(BF16), ffn_w_up=ffn_w_up.astype(BF16), ffn_w_down=ffn_w_down.astype(BF16),
        ln_mix_g=row(ln_mix_g), ln_mix_b=row(ln_mix_b), ln_ffn_g=row(ln_ffn_g), ln_ffn_b=row(ln_ffn_b),
    )
    return (_trunk(x_prompt, p), _trunk(x_sample, p))
```
